```python
import jax, jax.numpy as jnp
from jax import lax
import numpy as np

D_MODEL = 2048
BATCH = 4
SEQ = 2048
DEPTH = 1

RET_HEADS = 8
RET_DK = 128
RET_DV = 128
HG_HEADS = 8
HG_DK = 128
HG_DV = 128
SECTION_SIZES = (RET_HEADS * RET_DK, RET_HEADS * RET_DK, RET_HEADS * RET_DV, RET_HEADS * RET_DV,
                 HG_HEADS * HG_DK, HG_HEADS * HG_DK, HG_HEADS * HG_DV, HG_HEADS * HG_DV)
IN_WIDTH = sum(SECTION_SIZES)
MIX_WIDTH = RET_HEADS * RET_DV + HG_HEADS * HG_DV
CHUNK = 64
ROPE_BASE = 10000.0
N_EXPERTS = 256
TOP_K = 8
N_GROUPS = 8
TOPK_GROUPS = 4
EXPERT_FF = 512
SHARED_FF = 512
ROUTED_SCALE = 2.5
EXPERT_BLOCK = 128
NORM_EPS = 1e-5
DEEPNORM_ALPHA = (2.0 * DEPTH) ** 0.25
DEEPNORM_BETA = (8.0 * DEPTH) ** -0.25

kernel_name = "hybrid_retention_hgrn2_moe_deepnorm"


def layer_norm(x, gain, bias):
    xf = x.astype(jnp.float32)
    mu = jnp.mean(xf, axis=-1, keepdims=True)
    var = jnp.mean(jnp.square(xf - mu), axis=-1, keepdims=True)
    return ((xf - mu) * lax.rsqrt(var + NORM_EPS) * gain.astype(jnp.float32)
            + bias.astype(jnp.float32)).astype(x.dtype)


def head_norm(o, subtract_mean):
    if subtract_mean:
        o = o - jnp.mean(o, axis=-1, keepdims=True)
    return o * lax.rsqrt(jnp.mean(jnp.square(o), axis=-1, keepdims=True) + NORM_EPS)


def rotary(t, pos):
    half = t.shape[-1] // 2
    inv_freq = ROPE_BASE ** (-jnp.arange(half, dtype=jnp.float32) / half)
    ang = pos[:, None] * inv_freq[None, :]
    cos = jnp.cos(ang)[None, :, None, :].astype(t.dtype)
    sin = jnp.sin(ang)[None, :, None, :].astype(t.dtype)
    t1, t2 = t[..., :half], t[..., half:]
    return jnp.concatenate([t1 * cos - t2 * sin, t1 * sin + t2 * cos], axis=-1)


def chunked_linear_recurrence(q, k, v, log_a):
    b_, h_, t_, dk = q.shape
    dv = v.shape[-1]
    da = log_a.shape[-1]
    n_chunks = t_ // CHUNK

    def to_chunks(t):
        return jnp.moveaxis(t.astype(jnp.float32).reshape(b_, h_, n_chunks, CHUNK, t.shape[-1]), 2, 0)

    causal = jnp.tril(jnp.ones((CHUNK, CHUNK), dtype=bool))

    def step(state, inp):
        qc, kc, vc, lc = inp
        b = jnp.cumsum(lc, axis=-2)
        b_last = b[..., -1:, :]
        o_inter = jnp.einsum('bhtk,bhkv->bhtv', qc * jnp.exp(b), state)
        if da == 1:
            diff = b[..., :, None, 0] - b[..., None, :, 0]
            decay = jnp.exp(jnp.where(causal, diff, -jnp.inf))
            scores = jnp.einsum('bhtk,bhsk->bhts', qc, kc) * decay
        else:
            diff = b[..., :, None, :] - b[..., None, :, :]
            decay = jnp.exp(jnp.where(causal[..., None], diff, -jnp.inf))
            scores = jnp.einsum('bhtk,bhsk,bhtsk->bhts', qc, kc, decay)
        o = o_inter + jnp.einsum('bhts,bhsv->bhtv', scores, vc)
        new_state = (jnp.exp(jnp.swapaxes(b_last, -1, -2)) * state
                     + jnp.einsum('bhsk,bhsv->bhkv', kc * jnp.exp(b_last - b), vc))
        return new_state, o

    s0 = jnp.zeros((b_, h_, dk, dv), jnp.float32)
    _, o = lax.scan(step, s0, (to_chunks(q), to_chunks(k), to_chunks(v), to_chunks(log_a)))
    return jnp.moveaxis(o, 0, 2).reshape(b_, h_, t_, dv)


def split_heads(t, n_heads):
    b_, s_, w = t.shape
    return t.reshape(b_, s_, n_heads, w // n_heads)


def hybrid_mixer(x, pos, ret_log_decay, lower_bound, w_in, ret_gn_gain, hgrn_norm_gain, w_out):
    b_, s_, _ = x.shape
    proj = x @ w_in
    split_at = [int(i) for i in np.cumsum(SECTION_SIZES)[:-1]]
    rq, rk, rv, rg, hq, hf, hi, hg = jnp.split(proj, split_at, axis=-1)

    rq = rotary(split_heads(rq, RET_HEADS), pos)
    rk = rotary(split_heads(rk, RET_HEADS), pos) * (RET_DK ** -0.5)
    rv = split_heads(rv, RET_HEADS)
    log_a_ret = jnp.broadcast_to(ret_log_decay[None, :, None, None], (b_, RET_HEADS, s_, 1))
    ro = chunked_linear_recurrence(rq.transpose(0, 2, 1, 3), rk.transpose(0, 2, 1, 3),
                                   rv.transpose(0, 2, 1, 3), log_a_ret)
    ro = head_norm(ro.transpose(0, 2, 1, 3), True).reshape(b_, s_, RET_HEADS * RET_DV)
    ret_out = ro * ret_gn_gain.astype(jnp.float32) * jax.nn.silu(rg.astype(jnp.float32))

    lb = lower_bound[None, None, :]
    f = lb + (1.0 - lb) * jax.nn.sigmoid(hf.astype(jnp.float32))
    log_f = jnp.log(f)
    hk = 1.0 - f
    to_bhsd = lambda t: split_heads(t, HG_HEADS).transpose(0, 2, 1, 3)
    ho = chunked_linear_recurrence(to_bhsd(hq), to_bhsd(hk), to_bhsd(hi), to_bhsd(log_f))
    ho = head_norm(ho.transpose(0, 2, 1, 3), False).reshape(b_, s_, HG_HEADS * HG_DV)
    hg_out = ho * hgrn_norm_gain.astype(jnp.float32) * jax.nn.silu(hg.astype(jnp.float32))

    mixed = jnp.concatenate([ret_out, hg_out], axis=-1).astype(x.dtype)
    return mixed @ w_out


def moe_ffn(h, w_router, router_bias, w_gate, w_up, w_down, ws_gate, ws_up, ws_down):
    b_, s_, d = h.shape
    n_tok = b_ * s_
    hf = h.reshape(n_tok, d)
    scores = jax.nn.sigmoid((hf @ w_router).astype(jnp.float32))
    sel = scores + router_bias.astype(jnp.float32)
    grp = sel.reshape(n_tok, N_GROUPS, N_EXPERTS // N_GROUPS)
    group_score = jnp.sum(lax.top_k(grp, 2)[0], axis=-1)
    _, gidx = lax.top_k(group_score, TOPK_GROUPS)
    gmask = jnp.any(gidx[..., None] == jnp.arange(N_GROUPS), axis=-2)
    emask = jnp.repeat(gmask, N_EXPERTS // N_GROUPS, axis=-1)
    _, eidx = lax.top_k(jnp.where(emask, sel, -jnp.inf), TOP_K)
    gate = jnp.take_along_axis(scores, eidx, axis=-1)
    gate = gate / jnp.sum(gate, axis=-1, keepdims=True) * ROUTED_SCALE

    n_assign = n_tok * TOP_K
    e_flat = eidx.reshape(n_assign)
    tok_flat = jnp.arange(n_assign, dtype=jnp.int32) // TOP_K
    w_flat = gate.reshape(n_assign)
    order = jnp.argsort(e_flat)
    e_sorted = e_flat[order]
    counts = jnp.bincount(e_flat, length=N_EXPERTS)
    padded = (counts + EXPERT_BLOCK - 1) // EXPERT_BLOCK * EXPERT_BLOCK
    ends = jnp.cumsum(padded)
    pad_start = ends - padded
    seg_start = jnp.cumsum(counts) - counts
    dest = pad_start[e_sorted] + jnp.arange(n_assign) - seg_start[e_sorted]
    n_blocks = -(-n_assign // EXPERT_BLOCK) + N_EXPERTS
    n_rows = n_blocks * EXPERT_BLOCK
    row_tok = jnp.full((n_rows,), n_tok, jnp.int32).at[dest].set(tok_flat[order])
    row_w = jnp.zeros((n_rows,), jnp.float32).at[dest].set(w_flat[order])
    block_e = jnp.minimum(jnp.searchsorted(ends, jnp.arange(n_blocks) * EXPERT_BLOCK, side='right'),
                          N_EXPERTS - 1)
    x_rows = jnp.concatenate([hf, jnp.zeros((1, d), hf.dtype)], axis=0)[row_tok]
    x_rows = x_rows.reshape(n_blocks, EXPERT_BLOCK, d)

    def expert_block(args):
        xb, e = args
        return (jax.nn.silu(xb @ w_gate[e]) * (xb @ w_up[e])) @ w_down[e]

    y_rows = lax.map(expert_block, (x_rows, block_e)).reshape(n_rows, d)
    routed = jnp.zeros((n_tok + 1, d), h.dtype).at[row_tok].add(
        y_rows * row_w[:, None].astype(h.dtype))[:n_tok]
    shared = (jax.nn.silu(hf @ ws_gate) * (hf @ ws_up)) @ ws_down
    return (routed + shared).reshape(b_, s_, d)


def setup_inputs(seed: int = 0) -> dict:
    key = jax.random.key(seed)
    ks = jax.random.split(key, 20)
    f32 = jnp.float32
    beta = DEEPNORM_BETA
    col_scale = jnp.concatenate([
        jnp.full((n,), beta if i in (2, 6) else 1.0, f32) for i, n in enumerate(SECTION_SIZES)])
    nrm = lambda k, shape: jax.random.normal(k, shape, f32)
    return {
        "x": nrm(ks[0], (BATCH, SEQ, D_MODEL)),
        "w_in": nrm(ks[1], (DEPTH, D_MODEL, IN_WIDTH)) * (D_MODEL ** -0.5) * col_scale,
        "ret_gn_gain": 1.0 + 0.02 * nrm(ks[2], (DEPTH, RET_HEADS * RET_DV)),
        "hgrn_lb_logits": 0.5 * nrm(ks[3], (DEPTH + 1, HG_HEADS * HG_DK)),
        "hgrn_norm_gain": 1.0 + 0.02 * nrm(ks[4], (DEPTH, HG_HEADS * HG_DV)),
        "w_out": nrm(ks[5], (DEPTH, MIX_WIDTH, D_MODEL)) * (MIX_WIDTH ** -0.5) * beta,
        "ln1_gain": 1.0 + 0.02 * nrm(ks[6], (DEPTH, D_MODEL)),
        "ln1_bias": 0.02 * nrm(ks[7], (DEPTH, D_MODEL)),
        "w_router": nrm(ks[8], (DEPTH, D_MODEL, N_EXPERTS)) * (D_MODEL ** -0.5),
        "router_bias": 0.01 * nrm(ks[9], (DEPTH, N_EXPERTS)),
        "w_gate": nrm(ks[10], (DEPTH, N_EXPERTS, D_MODEL, EXPERT_FF)) * (D_MODEL ** -0.5),
        "w_up": nrm(ks[11], (DEPTH, N_EXPERTS, D_MODEL, EXPERT_FF)) * (D_MODEL ** -0.5) * beta,
        "w_down": nrm(ks[12], (DEPTH, N_EXPERTS, EXPERT_FF, D_MODEL)) * (EXPERT_FF ** -0.5) * beta,
        "ws_gate": nrm(ks[13], (DEPTH, D_MODEL, SHARED_FF)) * (D_MODEL ** -0.5),
        "ws_up": nrm(ks[14], (DEPTH, D_MODEL, SHARED_FF)) * (D_MODEL ** -0.5) * beta,
        "ws_down": nrm(ks[15], (DEPTH, SHARED_FF, D_MODEL)) * (SHARED_FF ** -0.5) * beta,
        "ln2_gain": 1.0 + 0.02 * nrm(ks[16], (DEPTH, D_MODEL)),
        "ln2_bias": 0.02 * nrm(ks[17], (DEPTH, D_MODEL)),
    }


def reference(x, w_in, ret_gn_gain, hgrn_lb_logits, hgrn_norm_gain, w_out, ln1_gain, ln1_bias,
              w_router, router_bias, w_gate, w_up, w_down, ws_gate, ws_up, ws_down,
              ln2_gain, ln2_bias):
    seq = x.shape[1]
    pos = jnp.arange(seq, dtype=jnp.float32)
    ret_log_decay = jnp.log(1.0 - jnp.power(2.0, -5.0 - jnp.arange(RET_HEADS, dtype=jnp.float32)))
    lower_bounds = jnp.cumsum(jax.nn.softmax(hgrn_lb_logits.astype(jnp.float32), axis=0), axis=0)
    for l in range(DEPTH):
        mix = hybrid_mixer(x, pos, ret_log_decay, lower_bounds[l], w_in[l], ret_gn_gain[l],
                           hgrn_norm_gain[l], w_out[l])
        x = layer_norm(DEEPNORM_ALPHA * x + mix, ln1_gain[l], ln1_bias[l])
        ffn = moe_ffn(x, w_router[l], router_bias[l], w_gate[l], w_up[l], w_down[l],
                      ws_gate[l], ws_up[l], ws_down[l])
        x = layer_norm(DEEPNORM_ALPHA * x + ffn, ln2_gain[l], ln2_bias[l])
    return x
```

```python
import functools

import numpy as np
import jax
import jax.numpy as jnp
from jax import lax
from jax.experimental import pallas as pl
from jax.experimental.pallas import tpu as pltpu

D_MODEL = 2048
N_HEADS = 8
HEAD_DIM = 128
GROUP_WIDTH = N_HEADS * HEAD_DIM
N_SECTIONS = 8
ROPE_BASE = 10000.0
N_EXPERTS = 256
TOP_K = 8
N_GROUPS = 8
GROUP_SIZE = N_EXPERTS // N_GROUPS
TOPK_GROUPS = 4
EXPERT_FF = 512
SHARED_FF = 512
ROUTED_SCALE = 2.5
NORM_EPS = 1e-5
DEPTH = 1
DEEPNORM_ALPHA = (2.0 * DEPTH) ** 0.25

V7X_VMEM_LIMIT_BYTES = 56 * 1024 * 1024
LANES = 128
SUBLANES = 8

INPROJ_TM = 1024
MIX_TS = 512
RET_CHUNK = 128
HG_CHUNK = 64
HG_DIRECT = 8
OUTPROJ_TM = 256
ROUTER_TM = 256
EXPERT_TM = 256
COMBINE_TM = 128

F32 = jnp.float32
BF16 = jnp.bfloat16


def _sigmoid(v):
    return 1.0 / (1.0 + jnp.exp(-v))


def _silu(v):
    return v * _sigmoid(v)


def _cparams(semantics):
    return pltpu.CompilerParams(dimension_semantics=semantics,
                                vmem_limit_bytes=V7X_VMEM_LIMIT_BYTES)


def _inproj_kernel(x_ref, w_ref, cos_ref, sin_ref, lb_ref, p_ref, logf_ref):
    j = pl.program_id(1)
    acc = jnp.dot(x_ref[...], w_ref[...], preferred_element_type=F32)

    @pl.when(j < 2)
    def _rotary():
        scale = jnp.where(j == 1, HEAD_DIM ** -0.5, 1.0).astype(F32)
        cos = cos_ref[...]
        sin = sin_ref[...]
        for h in range(N_HEADS):
            t = acc[:, h * HEAD_DIM:(h + 1) * HEAD_DIM]
            r = pltpu.roll(t, HEAD_DIM // 2, axis=1)
            p_ref[:, h * HEAD_DIM:(h + 1) * HEAD_DIM] = ((t * cos + r * sin) * scale).astype(BF16)

    @pl.when(j == 5)
    def _forget():
        lb = lb_ref[...]
        f = lb + (1.0 - lb) * _sigmoid(acc)
        logf_ref[...] = jnp.log(f)
        p_ref[...] = (1.0 - f).astype(BF16)

    @pl.when(jnp.logical_and(j >= 2, j != 5))
    def _plain():
        p_ref[...] = acc.astype(BF16)


def _in_proj(x_bf, w_bf, cos_t, sin_t, lb, seq):
    n_tok = x_bf.shape[0]
    tm = INPROJ_TM
    pos_blocks = seq // tm
    return pl.pallas_call(
        _inproj_kernel,
        grid=(n_tok // tm, N_SECTIONS),
        in_specs=[
            pl.BlockSpec((tm, D_MODEL), lambda i, j: (i, 0)),
            pl.BlockSpec((D_MODEL, GROUP_WIDTH), lambda i, j: (0, j)),
            pl.BlockSpec((tm, HEAD_DIM), lambda i, j: (i % pos_blocks, 0)),
            pl.BlockSpec((tm, HEAD_DIM), lambda i, j: (i % pos_blocks, 0)),
            pl.BlockSpec((1, GROUP_WIDTH), lambda i, j: (0, 0)),
        ],
        out_specs=[
            pl.BlockSpec((tm, GROUP_WIDTH), lambda i, j: (i, j)),
            pl.BlockSpec((tm, GROUP_WIDTH), lambda i, j: (i, 0)),
        ],
        out_shape=[
            jax.ShapeDtypeStruct((n_tok, N_SECTIONS * GROUP_WIDTH), BF16),
            jax.ShapeDtypeStruct((n_tok, GROUP_WIDTH), F32),
        ],
        compiler_params=_cparams(("arbitrary", "arbitrary")),
        name="in_proj",
    )(x_bf, w_bf, cos_t, sin_t, lb)


def _ret_log_decay(h):
    return float(np.log(np.float32(1.0) - np.float32(2.0) ** np.float32(-5.0 - h)))


def _retention_kernel(q_ref, k_ref, v_ref, g_ref, gain_ref, o_ref, state_ref):
    c = RET_CHUNK

    @pl.when(pl.program_id(1) == 0)
    def _():
        state_ref[...] = jnp.zeros_like(state_ref)

    row = lax.broadcasted_iota(jnp.int32, (c, c), 0)
    col = lax.broadcasted_iota(jnp.int32, (c, c), 1)
    delta = (row - col).astype(F32)
    causal = row >= col
    rvec = lax.broadcasted_iota(jnp.int32, (c, 1), 0).astype(F32)

    for h in range(N_HEADS):
        ld = _ret_log_decay(h)
        decay = jnp.where(causal, jnp.exp(jnp.where(causal, delta, 0.0) * ld), 0.0)
        q_scale = jnp.exp((rvec + 1.0) * ld)
        k_scale = jnp.exp((c - 1.0 - rvec) * ld)
        chunk_decay = float(np.exp(np.float32(ld) * np.float32(c)))
        hs = slice(h * HEAD_DIM, (h + 1) * HEAD_DIM)
        gain = gain_ref[:, hs]

        def chunk_body(ci, carry, hs=hs, decay=decay, q_scale=q_scale, k_scale=k_scale,
                       chunk_decay=chunk_decay, gain=gain, h=h):
            rows = pl.ds(pl.multiple_of(ci * c, c), c)
            q = q_ref[rows, hs]
            k = k_ref[rows, hs]
            v = v_ref[rows, hs]
            state = state_ref[h]
            scores = lax.dot_general(q, k, (((1,), (1,)), ((), ())),
                                     preferred_element_type=F32) * decay
            o = jnp.dot(scores.astype(BF16), v, preferred_element_type=F32)
            o = o + q_scale * jnp.dot(q, state.astype(BF16), preferred_element_type=F32)
            k_dec = (k.astype(F32) * k_scale).astype(BF16)
            kv = lax.dot_general(k_dec, v, (((0,), (0,)), ((), ())),
                                 preferred_element_type=F32)
            state_ref[h] = chunk_decay * state + kv
            o = o - jnp.mean(o, axis=-1, keepdims=True)
            o = o * lax.rsqrt(jnp.mean(o * o, axis=-1, keepdims=True) + NORM_EPS)
            gate = g_ref[rows, hs].astype(F32)
            o_ref[rows, hs] = (o * gain * _silu(gate)).astype(BF16)
            return carry

        lax.fori_loop(0, MIX_TS // c, chunk_body, 0)


def _retention(p, gain, batch, seq):
    n_tok = p.shape[0]
    ts = MIX_TS
    tpb = seq // ts
    sec = lambda s: pl.BlockSpec((ts, GROUP_WIDTH), lambda b, t, s=s: (b * tpb + t, s))
    return pl.pallas_call(
        _retention_kernel,
        grid=(batch, tpb),
        in_specs=[sec(0), sec(1), sec(2), sec(3),
                  pl.BlockSpec((1, GROUP_WIDTH), lambda b, t: (0, 0))],
        out_specs=pl.BlockSpec((ts, GROUP_WIDTH), lambda b, t: (b * tpb + t, 0)),
        out_shape=jax.ShapeDtypeStruct((n_tok, GROUP_WIDTH), BF16),
        scratch_shapes=[pltpu.VMEM((N_HEADS, HEAD_DIM, HEAD_DIM), F32)],
        compiler_params=_cparams(("arbitrary", "arbitrary")),
        name="retention",
    )(p, p, p, p, gain)


def _hgrn2_kernel(q_ref, k_ref, v_ref, g_ref, logf_ref, gain_ref, o_ref, state_ref, cum_ref):
    c = HG_CHUNK
    ts = MIX_TS

    @pl.when(pl.program_id(1) == 0)
    def _():
        state_ref[...] = jnp.zeros_like(state_ref)

    trow = lax.broadcasted_iota(jnp.int32, (ts, 1), 0) % c
    cum = logf_ref[...]
    sh = 1
    while sh < c:
        cum = cum + jnp.where(trow >= sh, pltpu.roll(cum, sh, axis=0), 0.0)
        sh *= 2
    cum_ref[...] = cum

    crow = lax.broadcasted_iota(jnp.int32, (c, 1), 0)
    srow = lax.broadcasted_iota(jnp.int32, (c, c), 0)
    scol = lax.broadcasted_iota(jnp.int32, (c, c), 1)
    drow = lax.broadcasted_iota(jnp.int32, (HG_DIRECT, 1), 0)
    ones_bf = jnp.ones((HEAD_DIM, HEAD_DIM), BF16)
    n_direct = c // HG_DIRECT

    def chunk_body(ci, carry):
        rows = pl.ds(pl.multiple_of(ci * c, c), c)
        for h in range(N_HEADS):
            hs = slice(h * HEAD_DIM, (h + 1) * HEAD_DIM)
            q = q_ref[rows, hs].astype(F32)
            k = k_ref[rows, hs].astype(F32)
            v_bf = v_ref[rows, hs]
            v = v_bf.astype(F32)
            b = cum_ref[rows, hs]
            b_last = b[c - 1:c, :]
            state_t = state_ref[h]

            q_in = (q * jnp.exp(b)).astype(BF16)
            o = lax.dot_general(q_in, state_t.astype(BF16), (((1,), (1,)), ((), ())),
                                preferred_element_type=F32)

            scores = jnp.zeros((c, c), F32)
            level = 2 * HG_DIRECT
            while level <= c:
                half = level // 2
                anchors = []
                for blk in range(c // level):
                    a = blk * level + half - 1
                    anchors.append(jnp.broadcast_to(b[a:a + 1, :], (level, HEAD_DIM)))
                anchor = anchors[0] if len(anchors) == 1 else jnp.concatenate(anchors, axis=0)
                upper = (crow % level) >= half
                q_l = jnp.where(upper, q * jnp.exp(jnp.minimum(b - anchor, 0.0)), 0.0)
                k_l = jnp.where(upper, 0.0, k * jnp.exp(jnp.minimum(anchor - b, 0.0)))
                s_l = lax.dot_general(q_l.astype(BF16), k_l.astype(BF16),
                                      (((1,), (1,)), ((), ())), preferred_element_type=F32)
                same_block = (srow // level) == (scol // level)
                scores = scores + jnp.where(same_block, s_l, 0.0)
                level *= 2
            o = o + jnp.dot(scores.astype(BF16), v_bf, preferred_element_type=F32)

            prods = []
            for blk in range(n_direct):
                r0 = blk * HG_DIRECT
                qb = q[r0:r0 + HG_DIRECT, :]
                bb = b[r0:r0 + HG_DIRECT, :]
                for s in range(HG_DIRECT):
                    dec = jnp.exp(jnp.minimum(bb - b[r0 + s:r0 + s + 1, :], 0.0))
                    prods.append(jnp.where(drow >= s, qb * dec * k[r0 + s:r0 + s + 1, :], 0.0))
            prod = jnp.concatenate(prods, axis=0).astype(BF16)
            rowsum = jnp.dot(prod, ones_bf, preferred_element_type=F32)
            direct = []
            for blk in range(n_direct):
                r0 = blk * HG_DIRECT
                acc = jnp.zeros((HG_DIRECT, HEAD_DIM), F32)
                for s in range(HG_DIRECT):
                    p0 = (blk * HG_DIRECT + s) * HG_DIRECT
                    acc = acc + rowsum[p0:p0 + HG_DIRECT, :] * v[r0 + s:r0 + s + 1, :]
                direct.append(acc)
            o = o + jnp.concatenate(direct, axis=0)

            k_dec = (k * jnp.exp(b_last - b)).astype(BF16)
            vk = lax.dot_general(v_bf, k_dec, (((0,), (0,)), ((), ())),
                                 preferred_element_type=F32)
            state_ref[h] = state_t * jnp.exp(b_last) + vk

            o = o * lax.rsqrt(jnp.mean(o * o, axis=-1, keepdims=True) + NORM_EPS)
            gate = g_ref[rows, hs].astype(F32)
            o_ref[rows, hs] = (o * gain_ref[:, hs] * _silu(gate)).astype(BF16)
        return carry

    lax.fori_loop(0, ts // c, chunk_body, 0)


def _hgrn2(p, logf, gain, batch, seq):
    n_tok = p.shape[0]
    ts = MIX_TS
    tpb = seq // ts
    sec = lambda s: pl.BlockSpec((ts, GROUP_WIDTH), lambda b, t, s=s: (b * tpb + t, s))
    return pl.pallas_call(
        _hgrn2_kernel,
        grid=(batch, tpb),
        in_specs=[sec(4), sec(5), sec(6), sec(7),
                  pl.BlockSpec((ts, GROUP_WIDTH), lambda b, t: (b * tpb + t, 0)),
                  pl.BlockSpec((1, GROUP_WIDTH), lambda b, t: (0, 0))],
        out_specs=pl.BlockSpec((ts, GROUP_WIDTH), lambda b, t: (b * tpb + t, 0)),
        out_shape=jax.ShapeDtypeStruct((n_tok, GROUP_WIDTH), BF16),
        scratch_shapes=[pltpu.VMEM((N_HEADS, HEAD_DIM, HEAD_DIM), F32),
                        pltpu.VMEM((ts, GROUP_WIDTH), F32)],
        compiler_params=_cparams(("arbitrary", "arbitrary")),
        name="hgrn2",
    )(p, p, p, p, logf, gain)


def _layer_norm(z, gain, bias):
    mu = jnp.mean(z, axis=-1, keepdims=True)
    zc = z - mu
    var = jnp.mean(zc * zc, axis=-1, keepdims=True)
    return zc * lax.rsqrt(var + NORM_EPS) * gain + bias


def _outproj_kernel(ret_ref, hg_ref, w_ref, x_ref, gain_ref, bias_ref, o_ref):
    mix = jnp.dot(ret_ref[...], w_ref[0:GROUP_WIDTH, :], preferred_element_type=F32)
    mix = mix + jnp.dot(hg_ref[...], w_ref[GROUP_WIDTH:2 * GROUP_WIDTH, :],
                        preferred_element_type=F32)
    z = DEEPNORM_ALPHA * x_ref[...] + mix
    o_ref[...] = _layer_norm(z, gain_ref[...], bias_ref[...])


def _out_proj(ret_o, hg_o, w_bf, x2d, gain, bias):
    n_tok = x2d.shape[0]
    tm = OUTPROJ_TM
    row = lambda w: pl.BlockSpec((tm, w), lambda i: (i, 0))
    full = lambda r, w: pl.BlockSpec((r, w), lambda i: (0, 0))
    return pl.pallas_call(
        _outproj_kernel,
        grid=(n_tok // tm,),
        in_specs=[row(GROUP_WIDTH), row(GROUP_WIDTH), full(2 * GROUP_WIDTH, D_MODEL),
                  row(D_MODEL), full(1, D_MODEL), full(1, D_MODEL)],
        out_specs=row(D_MODEL),
        out_shape=jax.ShapeDtypeStruct((n_tok, D_MODEL), F32),
        compiler_params=_cparams(("arbitrary",)),
        name="out_proj",
    )(ret_o, hg_o, w_bf, x2d, gain, bias)


def _router_kernel(x_ref, w_ref, bias_ref, eidx_ref, gate_ref, rank_ref, count_ref, base_ref):
    tm = ROUTER_TM

    @pl.when(pl.program_id(0) == 0)
    def _():
        base_ref[...] = jnp.zeros_like(base_ref)

    logits = jnp.dot(x_ref[...], w_ref[...], preferred_element_type=F32,
                     precision=lax.Precision.HIGHEST)
    scores = _sigmoid(logits)
    sel = scores + bias_ref[...]
    lane = lax.broadcasted_iota(jnp.int32, (tm, N_EXPERTS), 1)
    grp = lane // GROUP_SIZE
    neg = -jnp.inf

    def first_argmax(vals):
        m = jnp.max(vals, axis=1, keepdims=True)
        idx = jnp.min(jnp.where(vals == m, lane, N_EXPERTS), axis=1, keepdims=True)
        return m, idx

    group_score = []
    for g in range(N_GROUPS):
        vals = jnp.where(grp == g, sel, neg)
        m1, i1 = first_argmax(vals)
        m2 = jnp.max(jnp.where(lane == i1, neg, vals), axis=1, keepdims=True)
        group_score.append(m1 + m2)

    masked = jnp.full((tm, N_EXPERTS), neg, F32)
    for g in range(N_GROUPS):
        beaten_by = jnp.zeros((tm, 1), jnp.int32)
        for o in range(N_GROUPS):
            if o == g:
                continue
            wins = (group_score[o] >= group_score[g]) if o < g else (group_score[o] > group_score[g])
            beaten_by = beaten_by + wins.astype(jnp.int32)
        keep = beaten_by < TOPK_GROUPS
        masked = jnp.where(jnp.logical_and(grp == g, keep), sel, masked)

    out_lane = lax.broadcasted_iota(jnp.int32, (tm, LANES), 1)
    eidx_out = jnp.zeros((tm, LANES), jnp.int32)
    gate_out = jnp.zeros((tm, LANES), F32)
    onehots = []
    cur = masked
    for kk in range(TOP_K):
        _, idx = first_argmax(cur)
        onehot = lane == idx
        gate_k = jnp.sum(jnp.where(onehot, scores, 0.0), axis=1, keepdims=True)
        cur = jnp.where(onehot, neg, cur)
        onehots.append(onehot)
        eidx_out = jnp.where(out_lane == kk, idx, eidx_out)
        gate_out = jnp.where(out_lane == kk, gate_k, gate_out)
    gate_sum = jnp.sum(gate_out, axis=1, keepdims=True)
    eidx_ref[...] = eidx_out
    gate_ref[...] = gate_out / gate_sum * ROUTED_SCALE

    multihot = jnp.zeros((tm, N_EXPERTS), F32)
    for onehot in onehots:
        multihot = multihot + onehot.astype(F32)
    trow = lax.broadcasted_iota(jnp.int32, (tm, tm), 0)
    tcol = lax.broadcasted_iota(jnp.int32, (tm, tm), 1)
    earlier = jnp.where(trow > tcol, 1.0, 0.0).astype(BF16)
    before = jnp.dot(earlier, multihot.astype(BF16), preferred_element_type=F32)
    pos = before + base_ref[...]
    rank_out = jnp.zeros((tm, LANES), F32)
    for kk, onehot in enumerate(onehots):
        rank_k = jnp.sum(jnp.where(onehot, pos, 0.0), axis=1, keepdims=True)
        rank_out = jnp.where(out_lane == kk, rank_k, rank_out)
    rank_ref[...] = rank_out.astype(jnp.int32)
    new_base = base_ref[...] + jnp.sum(multihot, axis=0, keepdims=True)
    base_ref[...] = new_base
    count_ref[...] = new_base.astype(jnp.int32)


def _router(x1, w_router, bias):
    n_tok = x1.shape[0]
    tm = ROUTER_TM
    row = lambda w: pl.BlockSpec((tm, w), lambda i: (i, 0))
    full = lambda r, w: pl.BlockSpec((r, w), lambda i: (0, 0))
    return pl.pallas_call(
        _router_kernel,
        grid=(n_tok // tm,),
        in_specs=[row(D_MODEL), full(D_MODEL, N_EXPERTS), full(1, N_EXPERTS)],
        out_specs=[row(LANES), row(LANES), row(LANES), full(1, N_EXPERTS)],
        out_shape=[jax.ShapeDtypeStruct((n_tok, LANES), jnp.int32),
                   jax.ShapeDtypeStruct((n_tok, LANES), F32),
                   jax.ShapeDtypeStruct((n_tok, LANES), jnp.int32),
                   jax.ShapeDtypeStruct((1, N_EXPERTS), jnp.int32)],
        scratch_shapes=[pltpu.VMEM((1, N_EXPERTS), F32)],
        compiler_params=_cparams(("arbitrary",)),
        name="router",
    )(x1, w_router, bias)


def _row_copy(x_hbm, xbuf, sem, tok, slot, r):
    return pltpu.make_async_copy(x_hbm.at[pl.ds(tok, 1), :], xbuf.at[slot, pl.ds(r, 1), :],
                                 sem.at[slot])


def _experts_kernel(tile_e_ref, tile_out_ref, n_valid_ref, row_tok_ref,
                    x_hbm, wg_ref, wu_ref, wd_ref, y_ref, xbuf, sem):
    tm = EXPERT_TM
    i = pl.program_id(0)
    n_tiles = pl.num_programs(0)
    n_valid = n_valid_ref[0]
    slot = i % 2

    def issue(tile, dst_slot):
        def body(r, carry):
            tok = row_tok_ref[tile * tm + r]
            _row_copy(x_hbm, xbuf, sem, tok, dst_slot, r).start()
            return carry
        lax.fori_loop(0, tm, body, 0, unroll=8)

    def wait_all(dst_slot):
        def body(r, carry):
            _row_copy(x_hbm, xbuf, sem, 0, dst_slot, r).wait()
            return carry
        lax.fori_loop(0, tm, body, 0, unroll=8)

    @pl.when(jnp.logical_and(i == 0, n_valid > 0))
    def _():
        issue(0, 0)

    @pl.when(i + 1 < n_valid)
    def _():
        issue(i + 1, 1 - slot)

    @pl.when(i < n_valid)
    def _():
        wait_all(slot)
        x = xbuf[slot].astype(BF16)
        g = jnp.dot(x, wg_ref[...].astype(BF16), preferred_element_type=F32)
        u = jnp.dot(x, wu_ref[...].astype(BF16), preferred_element_type=F32)
        hmid = (_silu(g) * u).astype(BF16)
        y_ref[...] = jnp.dot(hmid, wd_ref[...].astype(BF16), preferred_element_type=F32)


def _experts(tile_e, tile_out, n_valid, row_tok, x1, w_gate, w_up, w_down):
    n_tiles = tile_e.shape[0]
    tm = EXPERT_TM
    grid_spec = pltpu.PrefetchScalarGridSpec(
        num_scalar_prefetch=4,
        grid=(n_tiles,),
        in_specs=[
            pl.BlockSpec(memory_space=pl.ANY),
            pl.BlockSpec((None, D_MODEL, EXPERT_FF), lambda i, te, to, nv, rt: (te[i], 0, 0)),
            pl.BlockSpec((None, D_MODEL, EXPERT_FF), lambda i, te, to, nv, rt: (te[i], 0, 0)),
            pl.BlockSpec((None, EXPERT_FF, D_MODEL), lambda i, te, to, nv, rt: (te[i], 0, 0)),
        ],
        out_specs=pl.BlockSpec((tm, D_MODEL), lambda i, te, to, nv, rt: (to[i], 0)),
        scratch_shapes=[pltpu.VMEM((2, tm, D_MODEL), F32),
                        pltpu.SemaphoreType.DMA((2,))],
    )
    return pl.pallas_call(
        _experts_kernel,
        grid_spec=grid_spec,
        out_shape=jax.ShapeDtypeStruct((n_tiles * tm, D_MODEL), F32),
        compiler_params=_cparams(("arbitrary",)),
        name="experts",
    )(tile_e, tile_out, n_valid, row_tok, x1, w_gate, w_up, w_down)


def _y_copy(y_hbm, ybuf, sem, src_row, slot, kk, r):
    return pltpu.make_async_copy(y_hbm.at[pl.ds(src_row, 1), :],
                                 ybuf.at[slot, kk, pl.ds(r, 1), :], sem.at[slot])


def _combine_kernel(dest_ref, y_hbm, x_ref, gate_ref, wsg_ref, wsu_ref, wsd_ref,
                    gain_ref, bias_ref, o_ref, ybuf, sem):
    tm = COMBINE_TM
    i = pl.program_id(0)
    n_tiles = pl.num_programs(0)
    slot = i % 2

    def issue(tile, dst_slot):
        def body(r, carry):
            base = (tile * tm + r) * TOP_K
            for kk in range(TOP_K):
                _y_copy(y_hbm, ybuf, sem, dest_ref[base + kk], dst_slot, kk, r).start()
            return carry
        lax.fori_loop(0, tm, body, 0)

    def wait_all(dst_slot):
        def body(r, carry):
            for kk in range(TOP_K):
                _y_copy(y_hbm, ybuf, sem, 0, dst_slot, kk, r).wait()
            return carry
        lax.fori_loop(0, tm, body, 0)

    @pl.when(i == 0)
    def _():
        issue(0, 0)

    @pl.when(i + 1 < n_tiles)
    def _():
        issue(i + 1, 1 - slot)

    x = x_ref[...]
    xb = x.astype(BF16)
    g = jnp.dot(xb, wsg_ref[...], preferred_element_type=F32)
    u = jnp.dot(xb, wsu_ref[...], preferred_element_type=F32)
    shared = jnp.dot((_silu(g) * u).astype(BF16), wsd_ref[...], preferred_element_type=F32)

    wait_all(slot)
    gate = gate_ref[...]
    routed = jnp.zeros((tm, D_MODEL), F32)
    for kk in range(TOP_K):
        routed = routed + gate[:, kk:kk + 1] * ybuf[slot, kk]
    z = DEEPNORM_ALPHA * x + (routed + shared)
    o_ref[...] = _layer_norm(z, gain_ref[...], bias_ref[...])


def _combine(dest_flat, y_rows, x1, gate, wsg, wsu, wsd, gain, bias):
    n_tok = x1.shape[0]
    tm = COMBINE_TM
    row = lambda w: pl.BlockSpec((tm, w), lambda i, d: (i, 0))
    full = lambda r, w: pl.BlockSpec((r, w), lambda i, d: (0, 0))
    grid_spec = pltpu.PrefetchScalarGridSpec(
        num_scalar_prefetch=1,
        grid=(n_tok // tm,),
        in_specs=[pl.BlockSpec(memory_space=pl.ANY), row(D_MODEL), row(LANES),
                  full(D_MODEL, SHARED_FF), full(D_MODEL, SHARED_FF), full(SHARED_FF, D_MODEL),
                  full(1, D_MODEL), full(1, D_MODEL)],
        out_specs=row(D_MODEL),
        scratch_shapes=[pltpu.VMEM((2, TOP_K, tm, D_MODEL), F32),
                        pltpu.SemaphoreType.DMA((2,))],
    )
    return pl.pallas_call(
        _combine_kernel,
        grid_spec=grid_spec,
        out_shape=jax.ShapeDtypeStruct((n_tok, D_MODEL), F32),
        compiler_params=_cparams(("arbitrary",)),
        name="combine",
    )(dest_flat, y_rows, x1, gate, wsg, wsu, wsd, gain, bias)


def _rope_tables(seq):
    half = HEAD_DIM // 2
    pos = jnp.arange(seq, dtype=F32)
    inv_freq = ROPE_BASE ** (-jnp.arange(half, dtype=F32) / half)
    ang = pos[:, None] * inv_freq[None, :]
    cos, sin = jnp.cos(ang), jnp.sin(ang)
    return jnp.concatenate([cos, cos], axis=-1), jnp.concatenate([-sin, sin], axis=-1)


def _dispatch_plan(eidx, rank, counts, n_tok):
    tm = EXPERT_TM
    n_assign = n_tok * TOP_K
    n_tiles = n_assign // tm + N_EXPERTS
    padded = (counts + tm - 1) // tm * tm
    ends = jnp.cumsum(padded)
    pad_start = ends - padded
    dest = pad_start[eidx] + rank
    dest_flat = dest.reshape(n_assign).astype(jnp.int32)
    tok_flat = jnp.arange(n_assign, dtype=jnp.int32) // TOP_K
    row_tok = jnp.zeros((n_tiles * tm,), jnp.int32).at[dest_flat].set(tok_flat)
    n_valid = (ends[-1] // tm).astype(jnp.int32)
    tile_ids = jnp.arange(n_tiles, dtype=jnp.int32)
    tile_clamped = jnp.minimum(tile_ids, jnp.maximum(n_valid - 1, 0))
    tile_e = jnp.minimum(jnp.searchsorted(ends, tile_clamped * tm, side='right'),
                         N_EXPERTS - 1).astype(jnp.int32)
    return dest_flat, row_tok, tile_e, tile_clamped, n_valid.reshape(1)


def kernel(x, w_in, ret_gn_gain, hgrn_lb_logits, hgrn_norm_gain, w_out, ln1_gain, ln1_bias,
           w_router, router_bias, w_gate, w_up, w_down, ws_gate, ws_up, ws_down,
           ln2_gain, ln2_bias):
    batch, seq, d = x.shape
    n_tok = batch * seq
    cos_t, sin_t = _rope_tables(seq)
    lower_bounds = jnp.cumsum(jax.nn.softmax(hgrn_lb_logits.astype(F32), axis=0), axis=0)
    x2d = x.reshape(n_tok, d)
    for l in range(DEPTH):
        p, logf = _in_proj(x2d.astype(BF16), w_in[l].astype(BF16), cos_t, sin_t,
                           lower_bounds[l].reshape(1, GROUP_WIDTH), seq)
        ret_o = _retention(p, ret_gn_gain[l].reshape(1, GROUP_WIDTH), batch, seq)
        hg_o = _hgrn2(p, logf, hgrn_norm_gain[l].reshape(1, GROUP_WIDTH), batch, seq)
        x1 = _out_proj(ret_o, hg_o, w_out[l].astype(BF16), x2d,
                       ln1_gain[l].reshape(1, d), ln1_bias[l].reshape(1, d))
        eidx, gate, rank, counts = _router(x1, w_router[l], router_bias[l].reshape(1, N_EXPERTS))
        dest_flat, row_tok, tile_e, tile_out, n_valid = _dispatch_plan(
            eidx[:, :TOP_K], rank[:, :TOP_K], counts[0], n_tok)
        y_rows = _experts(tile_e, tile_out, n_valid, row_tok, x1, w_gate[l], w_up[l], w_down[l])
        x2d = _combine(dest_flat, y_rows, x1, gate, ws_gate[l].astype(BF16),
                       ws_up[l].astype(BF16), ws_down[l].astype(BF16),
                       ln2_gain[l].reshape(1, d), ln2_bias[l].reshape(1, d))
    return x2d.reshape(batch, seq, d)
```

```python
import functools

import numpy as np
import jax
import jax.numpy as jnp
from jax import lax
from jax.experimental import pallas as pl
from jax.experimental.pallas import tpu as pltpu

D_MODEL = 2048
N_HEADS = 8
HEAD_DIM = 128
GROUP_WIDTH = N_HEADS * HEAD_DIM
N_SECTIONS = 8
ROPE_BASE = 10000.0
N_EXPERTS = 256
TOP_K = 8
N_GROUPS = 8
GROUP_SIZE = N_EXPERTS // N_GROUPS
TOPK_GROUPS = 4
EXPERT_FF = 512
SHARED_FF = 512
ROUTED_SCALE = 2.5
NORM_EPS = 1e-5
DEPTH = 1
DEEPNORM_ALPHA = (2.0 * DEPTH) ** 0.25

V7X_VMEM_LIMIT_BYTES = 56 * 1024 * 1024
LANES = 128
SUBLANES = 8

INPROJ_TM = 1024
MIX_TS = 512
RET_CHUNK = 128
HG_CHUNK = 64
HG_DIRECT = 8
OUTPROJ_TM = 256
ROUTER_TM = 256
DISPATCH_TM = 256
EXPERT_TM = 256
ROW_PAD = SUBLANES
COMBINE_TM = 128

F32 = jnp.float32
BF16 = jnp.bfloat16


def _sigmoid(v):
    return 1.0 / (1.0 + jnp.exp(-v))


def _silu(v):
    return v * _sigmoid(v)


def _cparams(semantics):
    return pltpu.CompilerParams(dimension_semantics=semantics,
                                vmem_limit_bytes=V7X_VMEM_LIMIT_BYTES)


def _inproj_kernel(x_ref, w_ref, cos_ref, sin_ref, lb_ref, p_ref, logf_ref):
    j = pl.program_id(1)
    acc = jnp.dot(x_ref[...], w_ref[...], preferred_element_type=F32)

    @pl.when(j < 2)
    def _rotary():
        scale = jnp.where(j == 1, HEAD_DIM ** -0.5, 1.0).astype(F32)
        cos = cos_ref[...]
        sin = sin_ref[...]
        for h in range(N_HEADS):
            t = acc[:, h * HEAD_DIM:(h + 1) * HEAD_DIM]
            r = pltpu.roll(t, HEAD_DIM // 2, axis=1)
            p_ref[:, h * HEAD_DIM:(h + 1) * HEAD_DIM] = ((t * cos + r * sin) * scale).astype(BF16)

    @pl.when(j == 5)
    def _forget():
        lb = lb_ref[...]
        f = lb + (1.0 - lb) * _sigmoid(acc)
        logf_ref[...] = jnp.log(f)
        p_ref[...] = (1.0 - f).astype(BF16)

    @pl.when(jnp.logical_and(j >= 2, j != 5))
    def _plain():
        p_ref[...] = acc.astype(BF16)


def _in_proj(x_bf, w_bf, cos_t, sin_t, lb, seq):
    n_tok = x_bf.shape[0]
    tm = INPROJ_TM
    pos_blocks = seq // tm
    return pl.pallas_call(
        _inproj_kernel,
        grid=(n_tok // tm, N_SECTIONS),
        in_specs=[
            pl.BlockSpec((tm, D_MODEL), lambda i, j: (i, 0)),
            pl.BlockSpec((D_MODEL, GROUP_WIDTH), lambda i, j: (0, j)),
            pl.BlockSpec((tm, HEAD_DIM), lambda i, j: (i % pos_blocks, 0)),
            pl.BlockSpec((tm, HEAD_DIM), lambda i, j: (i % pos_blocks, 0)),
            pl.BlockSpec((1, GROUP_WIDTH), lambda i, j: (0, 0)),
        ],
        out_specs=[
            pl.BlockSpec((tm, GROUP_WIDTH), lambda i, j: (i, j)),
            pl.BlockSpec((tm, GROUP_WIDTH), lambda i, j: (i, 0)),
        ],
        out_shape=[
            jax.ShapeDtypeStruct((n_tok, N_SECTIONS * GROUP_WIDTH), BF16),
            jax.ShapeDtypeStruct((n_tok, GROUP_WIDTH), F32),
        ],
        compiler_params=_cparams(("arbitrary", "arbitrary")),
        name="in_proj",
    )(x_bf, w_bf, cos_t, sin_t, lb)


def _ret_log_decay(h):
    return float(np.log(np.float32(1.0) - np.float32(2.0) ** np.float32(-5.0 - h)))


def _retention_kernel(q_ref, k_ref, v_ref, g_ref, gain_ref, o_ref, state_ref):
    c = RET_CHUNK

    @pl.when(pl.program_id(1) == 0)
    def _():
        state_ref[...] = jnp.zeros_like(state_ref)

    row = lax.broadcasted_iota(jnp.int32, (c, c), 0)
    col = lax.broadcasted_iota(jnp.int32, (c, c), 1)
    delta = (row - col).astype(F32)
    causal = row >= col
    rvec = lax.broadcasted_iota(jnp.int32, (c, 1), 0).astype(F32)

    for h in range(N_HEADS):
        ld = _ret_log_decay(h)
        decay = jnp.where(causal, jnp.exp(jnp.where(causal, delta, 0.0) * ld), 0.0)
        q_scale = jnp.exp((rvec + 1.0) * ld)
        k_scale = jnp.exp((c - 1.0 - rvec) * ld)
        chunk_decay = float(np.exp(np.float32(ld) * np.float32(c)))
        hs = slice(h * HEAD_DIM, (h + 1) * HEAD_DIM)
        gain = gain_ref[:, hs]

        def chunk_body(ci, carry, hs=hs, decay=decay, q_scale=q_scale, k_scale=k_scale,
                       chunk_decay=chunk_decay, gain=gain, h=h):
            rows = pl.ds(pl.multiple_of(ci * c, c), c)
            q = q_ref[rows, hs]
            k = k_ref[rows, hs]
            v = v_ref[rows, hs]
            state = state_ref[h]
            scores = lax.dot_general(q, k, (((1,), (1,)), ((), ())),
                                     preferred_element_type=F32) * decay
            o = jnp.dot(scores.astype(BF16), v, preferred_element_type=F32)
            o = o + q_scale * jnp.dot(q, state.astype(BF16), preferred_element_type=F32)
            k_dec = (k.astype(F32) * k_scale).astype(BF16)
            kv = lax.dot_general(k_dec, v, (((0,), (0,)), ((), ())),
                                 preferred_element_type=F32)
            state_ref[h] = chunk_decay * state + kv
            o = o - jnp.mean(o, axis=-1, keepdims=True)
            o = o * lax.rsqrt(jnp.mean(o * o, axis=-1, keepdims=True) + NORM_EPS)
            gate = g_ref[rows, hs].astype(F32)
            o_ref[rows, hs] = (o * gain * _silu(gate)).astype(BF16)
            return carry

        lax.fori_loop(0, MIX_TS // c, chunk_body, 0)


def _retention(p, gain, batch, seq):
    n_tok = p.shape[0]
    ts = MIX_TS
    tpb = seq // ts
    sec = lambda s: pl.BlockSpec((ts, GROUP_WIDTH), lambda b, t, s=s: (b * tpb + t, s))
    return pl.pallas_call(
        _retention_kernel,
        grid=(batch, tpb),
        in_specs=[sec(0), sec(1), sec(2), sec(3),
                  pl.BlockSpec((1, GROUP_WIDTH), lambda b, t: (0, 0))],
        out_specs=pl.BlockSpec((ts, GROUP_WIDTH), lambda b, t: (b * tpb + t, 0)),
        out_shape=jax.ShapeDtypeStruct((n_tok, GROUP_WIDTH), BF16),
        scratch_shapes=[pltpu.VMEM((N_HEADS, HEAD_DIM, HEAD_DIM), F32)],
        compiler_params=_cparams(("arbitrary", "arbitrary")),
        name="retention",
    )(p, p, p, p, gain)


def _hgrn2_kernel(q_ref, k_ref, v_ref, g_ref, logf_ref, gain_ref, o_ref, state_ref, cum_ref):
    c = HG_CHUNK
    ts = MIX_TS

    @pl.when(pl.program_id(1) == 0)
    def _():
        state_ref[...] = jnp.zeros_like(state_ref)

    trow = lax.broadcasted_iota(jnp.int32, (ts, 1), 0) % c
    cum = logf_ref[...]
    sh = 1
    while sh < c:
        cum = cum + jnp.where(trow >= sh, pltpu.roll(cum, sh, axis=0), 0.0)
        sh *= 2
    cum_ref[...] = cum

    crow = lax.broadcasted_iota(jnp.int32, (c, 1), 0)
    srow = lax.broadcasted_iota(jnp.int32, (c, c), 0)
    scol = lax.broadcasted_iota(jnp.int32, (c, c), 1)
    drow = lax.broadcasted_iota(jnp.int32, (HG_DIRECT, 1), 0)
    ones_bf = jnp.ones((HEAD_DIM, HEAD_DIM), BF16)
    n_direct = c // HG_DIRECT

    def chunk_body(ci, carry):
        rows = pl.ds(pl.multiple_of(ci * c, c), c)
        for h in range(N_HEADS):
            hs = slice(h * HEAD_DIM, (h + 1) * HEAD_DIM)
            q = q_ref[rows, hs].astype(F32)
            k = k_ref[rows, hs].astype(F32)
            v_bf = v_ref[rows, hs]
            v = v_bf.astype(F32)
            b = cum_ref[rows, hs]
            b_last = b[c - 1:c, :]
            state_t = state_ref[h]

            q_in = (q * jnp.exp(b)).astype(BF16)
            o = lax.dot_general(q_in, state_t.astype(BF16), (((1,), (1,)), ((), ())),
                                preferred_element_type=F32)

            scores = jnp.zeros((c, c), F32)
            level = 2 * HG_DIRECT
            while level <= c:
                half = level // 2
                anchors = []
                for blk in range(c // level):
                    a = blk * level + half - 1
                    anchors.append(jnp.broadcast_to(b[a:a + 1, :], (level, HEAD_DIM)))
                anchor = anchors[0] if len(anchors) == 1 else jnp.concatenate(anchors, axis=0)
                upper = (crow % level) >= half
                q_l = jnp.where(upper, q * jnp.exp(jnp.minimum(b - anchor, 0.0)), 0.0)
                k_l = jnp.where(upper, 0.0, k * jnp.exp(jnp.minimum(anchor - b, 0.0)))
                s_l = lax.dot_general(q_l.astype(BF16), k_l.astype(BF16),
                                      (((1,), (1,)), ((), ())), preferred_element_type=F32)
                same_block = (srow // level) == (scol // level)
                scores = scores + jnp.where(same_block, s_l, 0.0)
                level *= 2
            o = o + jnp.dot(scores.astype(BF16), v_bf, preferred_element_type=F32)

            prods = []
            for blk in range(n_direct):
                r0 = blk * HG_DIRECT
                qb = q[r0:r0 + HG_DIRECT, :]
                bb = b[r0:r0 + HG_DIRECT, :]
                for s in range(HG_DIRECT):
                    dec = jnp.exp(jnp.minimum(bb - b[r0 + s:r0 + s + 1, :], 0.0))
                    prods.append(jnp.where(drow >= s, qb * dec * k[r0 + s:r0 + s + 1, :], 0.0))
            prod = jnp.concatenate(prods, axis=0).astype(BF16)
            rowsum = jnp.dot(prod, ones_bf, preferred_element_type=F32)
            direct = []
            for blk in range(n_direct):
                r0 = blk * HG_DIRECT
                acc = jnp.zeros((HG_DIRECT, HEAD_DIM), F32)
                for s in range(HG_DIRECT):
                    p0 = (blk * HG_DIRECT + s) * HG_DIRECT
                    acc = acc + rowsum[p0:p0 + HG_DIRECT, :] * v[r0 + s:r0 + s + 1, :]
                direct.append(acc)
            o = o + jnp.concatenate(direct, axis=0)

            k_dec = (k * jnp.exp(b_last - b)).astype(BF16)
            vk = lax.dot_general(v_bf, k_dec, (((0,), (0,)), ((), ())),
                                 preferred_element_type=F32)
            state_ref[h] = state_t * jnp.exp(b_last) + vk

            o = o * lax.rsqrt(jnp.mean(o * o, axis=-1, keepdims=True) + NORM_EPS)
            gate = g_ref[rows, hs].astype(F32)
            o_ref[rows, hs] = (o * gain_ref[:, hs] * _silu(gate)).astype(BF16)
        return carry

    lax.fori_loop(0, ts // c, chunk_body, 0)


def _hgrn2(p, logf, gain, batch, seq):
    n_tok = p.shape[0]
    ts = MIX_TS
    tpb = seq // ts
    sec = lambda s: pl.BlockSpec((ts, GROUP_WIDTH), lambda b, t, s=s: (b * tpb + t, s))
    return pl.pallas_call(
        _hgrn2_kernel,
        grid=(batch, tpb),
        in_specs=[sec(4), sec(5), sec(6), sec(7),
                  pl.BlockSpec((ts, GROUP_WIDTH), lambda b, t: (b * tpb + t, 0)),
                  pl.BlockSpec((1, GROUP_WIDTH), lambda b, t: (0, 0))],
        out_specs=pl.BlockSpec((ts, GROUP_WIDTH), lambda b, t: (b * tpb + t, 0)),
        out_shape=jax.ShapeDtypeStruct((n_tok, GROUP_WIDTH), BF16),
        scratch_shapes=[pltpu.VMEM((N_HEADS, HEAD_DIM, HEAD_DIM), F32),
                        pltpu.VMEM((ts, GROUP_WIDTH), F32)],
        compiler_params=_cparams(("arbitrary", "arbitrary")),
        name="hgrn2",
    )(p, p, p, p, logf, gain)


def _layer_norm(z, gain, bias):
    mu = jnp.mean(z, axis=-1, keepdims=True)
    zc = z - mu
    var = jnp.mean(zc * zc, axis=-1, keepdims=True)
    return zc * lax.rsqrt(var + NORM_EPS) * gain + bias


def _outproj_kernel(ret_ref, hg_ref, w_ref, x_ref, gain_ref, bias_ref, o_ref):
    mix = jnp.dot(ret_ref[...], w_ref[0:GROUP_WIDTH, :], preferred_element_type=F32)
    mix = mix + jnp.dot(hg_ref[...], w_ref[GROUP_WIDTH:2 * GROUP_WIDTH, :],
                        preferred_element_type=F32)
    z = DEEPNORM_ALPHA * x_ref[...] + mix
    o_ref[...] = _layer_norm(z, gain_ref[...], bias_ref[...])


def _out_proj(ret_o, hg_o, w_bf, x2d, gain, bias):
    n_tok = x2d.shape[0]
    tm = OUTPROJ_TM
    row = lambda w: pl.BlockSpec((tm, w), lambda i: (i, 0))
    full = lambda r, w: pl.BlockSpec((r, w), lambda i: (0, 0))
    return pl.pallas_call(
        _outproj_kernel,
        grid=(n_tok // tm,),
        in_specs=[row(GROUP_WIDTH), row(GROUP_WIDTH), full(2 * GROUP_WIDTH, D_MODEL),
                  row(D_MODEL), full(1, D_MODEL), full(1, D_MODEL)],
        out_specs=row(D_MODEL),
        out_shape=jax.ShapeDtypeStruct((n_tok, D_MODEL), F32),
        compiler_params=_cparams(("arbitrary",)),
        name="out_proj",
    )(ret_o, hg_o, w_bf, x2d, gain, bias)


def _router_kernel(x_ref, w_ref, bias_ref, gate_ref, dest_ref, count_ref, seg_ref,
                   base_ref, eidx_scr, rank_scr):
    tm = ROUTER_TM
    step = pl.program_id(0)
    tok_rows = pl.ds(pl.multiple_of(step * tm, tm), tm)

    @pl.when(step == 0)
    def _():
        base_ref[...] = jnp.zeros_like(base_ref)

    logits = jnp.dot(x_ref[...], w_ref[...], preferred_element_type=F32,
                     precision=lax.Precision.HIGHEST)
    scores = _sigmoid(logits)
    sel = scores + bias_ref[...]
    lane = lax.broadcasted_iota(jnp.int32, (tm, N_EXPERTS), 1)
    grp = lane // GROUP_SIZE
    neg = -jnp.inf

    def first_argmax(vals):
        m = jnp.max(vals, axis=1, keepdims=True)
        idx = jnp.min(jnp.where(vals == m, lane, N_EXPERTS), axis=1, keepdims=True)
        return m, idx

    group_score = []
    for g in range(N_GROUPS):
        vals = jnp.where(grp == g, sel, neg)
        m1, i1 = first_argmax(vals)
        m2 = jnp.max(jnp.where(lane == i1, neg, vals), axis=1, keepdims=True)
        group_score.append(m1 + m2)

    masked = jnp.full((tm, N_EXPERTS), neg, F32)
    for g in range(N_GROUPS):
        beaten_by = jnp.zeros((tm, 1), jnp.int32)
        for o in range(N_GROUPS):
            if o == g:
                continue
            wins = (group_score[o] >= group_score[g]) if o < g else (group_score[o] > group_score[g])
            beaten_by = beaten_by + wins.astype(jnp.int32)
        keep = beaten_by < TOPK_GROUPS
        masked = jnp.where(jnp.logical_and(grp == g, keep), sel, masked)

    out_lane = lax.broadcasted_iota(jnp.int32, (tm, LANES), 1)
    eidx_out = jnp.zeros((tm, LANES), jnp.int32)
    gate_out = jnp.zeros((tm, LANES), F32)
    onehots = []
    cur = masked
    for kk in range(TOP_K):
        _, idx = first_argmax(cur)
        onehot = lane == idx
        gate_k = jnp.sum(jnp.where(onehot, scores, 0.0), axis=1, keepdims=True)
        cur = jnp.where(onehot, neg, cur)
        onehots.append(onehot)
        eidx_out = jnp.where(out_lane == kk, idx, eidx_out)
        gate_out = jnp.where(out_lane == kk, gate_k, gate_out)
    gate_sum = jnp.sum(gate_out, axis=1, keepdims=True)
    eidx_scr[tok_rows, :] = eidx_out
    gate_ref[...] = gate_out / gate_sum * ROUTED_SCALE

    multihot = jnp.zeros((tm, N_EXPERTS), F32)
    for onehot in onehots:
        multihot = multihot + onehot.astype(F32)
    trow = lax.broadcasted_iota(jnp.int32, (tm, tm), 0)
    tcol = lax.broadcasted_iota(jnp.int32, (tm, tm), 1)
    earlier = jnp.where(trow > tcol, 1.0, 0.0).astype(BF16)
    before = jnp.dot(earlier, multihot.astype(BF16), preferred_element_type=F32)
    pos = before + base_ref[...]
    rank_out = jnp.zeros((tm, LANES), F32)
    for kk, onehot in enumerate(onehots):
        rank_k = jnp.sum(jnp.where(onehot, pos, 0.0), axis=1, keepdims=True)
        rank_out = jnp.where(out_lane == kk, rank_k, rank_out)
    rank_scr[tok_rows, :] = rank_out.astype(jnp.int32)
    new_base = base_ref[...] + jnp.sum(multihot, axis=0, keepdims=True)
    base_ref[...] = new_base

    @pl.when(step == pl.num_programs(0) - 1)
    def _():
        counts = new_base
        padded = jnp.floor((counts + (ROW_PAD - 1)) / ROW_PAD) * ROW_PAD
        hi = jnp.floor(padded / 256.0)
        lo = padded - 256.0 * hi
        erow = lax.broadcasted_iota(jnp.int32, (N_EXPERTS, N_EXPERTS), 0)
        ecol = lax.broadcasted_iota(jnp.int32, (N_EXPERTS, N_EXPERTS), 1)
        before_e = jnp.where(erow < ecol, 1.0, 0.0).astype(BF16)
        bcast = lambda v: jnp.broadcast_to(v, (SUBLANES, N_EXPERTS)).astype(BF16)
        seg = (256.0 * jnp.dot(bcast(hi), before_e, preferred_element_type=F32)
               + jnp.dot(bcast(lo), before_e, preferred_element_type=F32))[0:1, :]
        count_ref[...] = counts.astype(jnp.int32)
        seg_ref[...] = seg.astype(jnp.int32)

        def tile_body(t, carry):
            rows = pl.ds(pl.multiple_of(t * tm, tm), tm)
            eidx = eidx_scr[rows, :]
            dest = jnp.zeros((tm, LANES), F32)
            for kk in range(TOP_K):
                hit = lane == eidx[:, kk:kk + 1]
                start_k = jnp.sum(jnp.where(hit, seg, 0.0), axis=1, keepdims=True)
                dest = jnp.where(out_lane == kk, start_k, dest)
            dest_ref[rows, :] = dest.astype(jnp.int32) + rank_scr[rows, :]
            return carry

        lax.fori_loop(0, pl.num_programs(0), tile_body, 0)


def _router(x1, w_router, bias):
    n_tok = x1.shape[0]
    tm = ROUTER_TM
    row = lambda w: pl.BlockSpec((tm, w), lambda i: (i, 0))
    full = lambda r, w: pl.BlockSpec((r, w), lambda i: (0, 0))
    return pl.pallas_call(
        _router_kernel,
        grid=(n_tok // tm,),
        in_specs=[row(D_MODEL), full(D_MODEL, N_EXPERTS), full(1, N_EXPERTS)],
        out_specs=[row(LANES), full(n_tok, LANES), full(1, N_EXPERTS), full(1, N_EXPERTS)],
        out_shape=[jax.ShapeDtypeStruct((n_tok, LANES), F32),
                   jax.ShapeDtypeStruct((n_tok, LANES), jnp.int32),
                   jax.ShapeDtypeStruct((1, N_EXPERTS), jnp.int32),
                   jax.ShapeDtypeStruct((1, N_EXPERTS), jnp.int32)],
        scratch_shapes=[pltpu.VMEM((1, N_EXPERTS), F32),
                        pltpu.VMEM((n_tok, LANES), jnp.int32),
                        pltpu.VMEM((n_tok, LANES), jnp.int32)],
        compiler_params=_cparams(("arbitrary",)),
        name="router",
    )(x1, w_router, bias)


def _sorted_rows(n_tok):
    return n_tok * TOP_K + N_EXPERTS * ROW_PAD + EXPERT_TM


def _padded_count(count):
    return (count + (ROW_PAD - 1)) // ROW_PAD * ROW_PAD


def _dispatch_row_copy(x_ref, xs_hbm, sem, row, dst_row):
    return pltpu.make_async_copy(x_ref.at[pl.ds(row, 1), :], xs_hbm.at[pl.ds(dst_row, 1), :], sem)


def _dispatch_kernel(dest_ref, seg_ref, count_ref, x_ref, xs_hbm, zero_buf, sem, zsem):
    tm = DISPATCH_TM
    i = pl.program_id(0)
    group = SUBLANES

    @pl.when(i == 0)
    def _():
        zero_buf[...] = jnp.zeros_like(zero_buf)

        def pad_copy(e, j):
            row = seg_ref[e] + count_ref[e] + j
            return pltpu.make_async_copy(zero_buf.at[pl.ds(0, 1), :], xs_hbm.at[pl.ds(row, 1), :], zsem)

        def pad_body(e, carry):
            n_pad = _padded_count(count_ref[e]) - count_ref[e]
            for j in range(ROW_PAD - 1):
                @pl.when(j < n_pad)
                def _(j=j):
                    pad_copy(e, j).start()
            for j in range(ROW_PAD - 1):
                @pl.when(j < n_pad)
                def _(j=j):
                    pad_copy(e, j).wait()
            return carry

        lax.fori_loop(0, N_EXPERTS, pad_body, 0)
        last = N_EXPERTS - 1
        n_rows = xs_hbm.shape[0]
        total = seg_ref[last] + _padded_count(count_ref[last])

        def tail_body(j, carry):
            row0 = pl.multiple_of(jnp.minimum(total + j * EXPERT_TM, n_rows - EXPERT_TM), ROW_PAD)
            tail = pltpu.make_async_copy(zero_buf, xs_hbm.at[pl.ds(row0, EXPERT_TM), :], zsem)
            tail.start()
            tail.wait()
            return carry

        lax.fori_loop(0, (n_rows - total + EXPERT_TM - 1) // EXPERT_TM, tail_body, 0)

    def issue_body(g, carry):
        for rr in range(group):
            row = g * group + rr
            base = (i * tm + row) * TOP_K
            for kk in range(TOP_K):
                _dispatch_row_copy(x_ref, xs_hbm, sem, row, dest_ref[base + kk]).start()
        return carry

    lax.fori_loop(0, tm // group, issue_body, 0)

    def wait_body(g, carry):
        for _ in range(group * TOP_K):
            _dispatch_row_copy(x_ref, xs_hbm, sem, 0, 0).wait()
        return carry

    lax.fori_loop(0, tm // group, wait_body, 0)


def _dispatch(dest_flat, seg, counts, x1):
    n_tok = x1.shape[0]
    tm = DISPATCH_TM
    grid_spec = pltpu.PrefetchScalarGridSpec(
        num_scalar_prefetch=3,
        grid=(n_tok // tm,),
        in_specs=[pl.BlockSpec((tm, D_MODEL), lambda i, d, s, c: (i, 0))],
        out_specs=pl.BlockSpec(memory_space=pl.ANY),
        scratch_shapes=[pltpu.VMEM((EXPERT_TM, D_MODEL), F32),
                        pltpu.SemaphoreType.DMA(()),
                        pltpu.SemaphoreType.DMA(())],
    )
    return pl.pallas_call(
        _dispatch_kernel,
        grid_spec=grid_spec,
        out_shape=jax.ShapeDtypeStruct((_sorted_rows(n_tok), D_MODEL), F32),
        compiler_params=_cparams(("arbitrary",)),
        name="dispatch",
    )(dest_flat, seg, counts, x1)


def _experts_kernel(seg_ref, count_ref, xs_hbm, wg_ref, wu_ref, wd_ref, y_hbm,
                    xbuf, ybuf, in_sem, out_sem, pending_ref):
    tm = EXPERT_TM
    e = pl.program_id(0)
    n_experts = pl.num_programs(0)
    n_rows = y_hbm.shape[0]
    count = count_ref[e]
    seg = seg_ref[e]
    n_chunks = (count + tm - 1) // tm

    def x_copy(row0, slot):
        return pltpu.make_async_copy(xs_hbm.at[pl.ds(pl.multiple_of(row0, ROW_PAD), tm), :],
                                     xbuf.at[slot], in_sem.at[slot])

    def y_copy(row0, slot):
        return pltpu.make_async_copy(ybuf.at[slot],
                                     y_hbm.at[pl.ds(pl.multiple_of(row0, ROW_PAD), tm), :], out_sem)

    def drain_output():
        @pl.when(pending_ref[0] == 1)
        def _():
            y_copy(0, 0).wait()
            pending_ref[0] = 0

    @pl.when(e == 0)
    def _():
        pending_ref[0] = 0
        pending_ref[1] = 0

        @pl.when(count > 0)
        def _():
            x_copy(seg, 0).start()

    def chunk_body(c, carry):
        slot = c % 2

        @pl.when(c + 1 < n_chunks)
        def _():
            x_copy(seg + (c + 1) * tm, 1 - slot).start()

        x_copy(0, slot).wait()
        x = xbuf[slot].astype(BF16)
        g = jnp.dot(x, wg_ref[...].astype(BF16), preferred_element_type=F32)
        u = jnp.dot(x, wu_ref[...].astype(BF16), preferred_element_type=F32)
        hmid = (_silu(g) * u).astype(BF16)
        yslot = pending_ref[1]
        ybuf[yslot] = jnp.dot(hmid, wd_ref[...].astype(BF16), preferred_element_type=F32)
        drain_output()
        y_copy(seg + c * tm, yslot).start()
        pending_ref[0] = 1
        pending_ref[1] = 1 - yslot
        return carry

    lax.fori_loop(0, n_chunks, chunk_body, 0)

    @pl.when(e + 1 < n_experts)
    def _():
        @pl.when(count_ref[jnp.minimum(e + 1, n_experts - 1)] > 0)
        def _():
            x_copy(seg_ref[jnp.minimum(e + 1, n_experts - 1)], 0).start()

    @pl.when(e == n_experts - 1)
    def _():
        drain_output()
        total = seg + _padded_count(count)
        ybuf[0] = jnp.zeros((tm, D_MODEL), F32)
        n_fill = (n_rows - total + tm - 1) // tm

        def fill_body(j, carry):
            fill = y_copy(jnp.minimum(total + j * tm, n_rows - tm), 0)
            fill.start()
            fill.wait()
            return carry

        lax.fori_loop(0, n_fill, fill_body, 0)


def _experts(seg, counts, xs, w_gate, w_up, w_down):
    n_rows = xs.shape[0]
    tm = EXPERT_TM
    wspec = lambda r, c: pl.BlockSpec((None, r, c), lambda e, s, n: (e, 0, 0))
    grid_spec = pltpu.PrefetchScalarGridSpec(
        num_scalar_prefetch=2,
        grid=(N_EXPERTS,),
        in_specs=[pl.BlockSpec(memory_space=pl.ANY),
                  wspec(D_MODEL, EXPERT_FF), wspec(D_MODEL, EXPERT_FF), wspec(EXPERT_FF, D_MODEL)],
        out_specs=pl.BlockSpec(memory_space=pl.ANY),
        scratch_shapes=[pltpu.VMEM((2, tm, D_MODEL), F32),
                        pltpu.VMEM((2, tm, D_MODEL), F32),
                        pltpu.SemaphoreType.DMA((2,)),
                        pltpu.SemaphoreType.DMA(()),
                        pltpu.SMEM((2,), jnp.int32)],
    )
    return pl.pallas_call(
        _experts_kernel,
        grid_spec=grid_spec,
        out_shape=jax.ShapeDtypeStruct((n_rows, D_MODEL), F32),
        compiler_params=_cparams(("arbitrary",)),
        name="experts",
    )(seg, counts, xs, w_gate, w_up, w_down)


def _y_copy(y_hbm, ybuf, sem, src_row, slot, kk, r):
    return pltpu.make_async_copy(y_hbm.at[pl.ds(src_row, 1), :],
                                 ybuf.at[slot, kk, pl.ds(r, 1), :], sem.at[slot])


def _combine_kernel(dest_ref, y_hbm, x_ref, gate_ref, wsg_ref, wsu_ref, wsd_ref,
                    gain_ref, bias_ref, o_ref, ybuf, sem):
    tm = COMBINE_TM
    i = pl.program_id(0)
    n_tiles = pl.num_programs(0)
    slot = i % 2

    def issue(tile, dst_slot):
        def body(r, carry):
            base = (tile * tm + r) * TOP_K
            for kk in range(TOP_K):
                _y_copy(y_hbm, ybuf, sem, dest_ref[base + kk], dst_slot, kk, r).start()
            return carry
        lax.fori_loop(0, tm, body, 0)

    def wait_all(dst_slot):
        def body(r, carry):
            for kk in range(TOP_K):
                _y_copy(y_hbm, ybuf, sem, 0, dst_slot, kk, r).wait()
            return carry
        lax.fori_loop(0, tm, body, 0)

    @pl.when(i == 0)
    def _():
        issue(0, 0)

    @pl.when(i + 1 < n_tiles)
    def _():
        issue(i + 1, 1 - slot)

    x = x_ref[...]
    xb = x.astype(BF16)
    g = jnp.dot(xb, wsg_ref[...], preferred_element_type=F32)
    u = jnp.dot(xb, wsu_ref[...], preferred_element_type=F32)
    shared = jnp.dot((_silu(g) * u).astype(BF16), wsd_ref[...], preferred_element_type=F32)

    wait_all(slot)
    gate = gate_ref[...]
    routed = jnp.zeros((tm, D_MODEL), F32)
    for kk in range(TOP_K):
        routed = routed + gate[:, kk:kk + 1] * ybuf[slot, kk]
    z = DEEPNORM_ALPHA * x + (routed + shared)
    o_ref[...] = _layer_norm(z, gain_ref[...], bias_ref[...])


def _combine(dest_flat, y_rows, x1, gate, wsg, wsu, wsd, gain, bias):
    n_tok = x1.shape[0]
    tm = COMBINE_TM
    row = lambda w: pl.BlockSpec((tm, w), lambda i, d: (i, 0))
    full = lambda r, w: pl.BlockSpec((r, w), lambda i, d: (0, 0))
    grid_spec = pltpu.PrefetchScalarGridSpec(
        num_scalar_prefetch=1,
        grid=(n_tok // tm,),
        in_specs=[pl.BlockSpec(memory_space=pl.ANY), row(D_MODEL), row(LANES),
                  full(D_MODEL, SHARED_FF), full(D_MODEL, SHARED_FF), full(SHARED_FF, D_MODEL),
                  full(1, D_MODEL), full(1, D_MODEL)],
        out_specs=row(D_MODEL),
        scratch_shapes=[pltpu.VMEM((2, TOP_K, tm, D_MODEL), F32),
                        pltpu.SemaphoreType.DMA((2,))],
    )
    return pl.pallas_call(
        _combine_kernel,
        grid_spec=grid_spec,
        out_shape=jax.ShapeDtypeStruct((n_tok, D_MODEL), F32),
        compiler_params=_cparams(("arbitrary",)),
        name="combine",
    )(dest_flat, y_rows, x1, gate, wsg, wsu, wsd, gain, bias)


def _rope_tables(seq):
    half = HEAD_DIM // 2
    pos = jnp.arange(seq, dtype=F32)
    inv_freq = ROPE_BASE ** (-jnp.arange(half, dtype=F32) / half)
    ang = pos[:, None] * inv_freq[None, :]
    cos, sin = jnp.cos(ang), jnp.sin(ang)
    return jnp.concatenate([cos, cos], axis=-1), jnp.concatenate([-sin, sin], axis=-1)


def kernel(x, w_in, ret_gn_gain, hgrn_lb_logits, hgrn_norm_gain, w_out, ln1_gain, ln1_bias,
           w_router, router_bias, w_gate, w_up, w_down, ws_gate, ws_up, ws_down,
           ln2_gain, ln2_bias):
    batch, seq, d = x.shape
    n_tok = batch * seq
    cos_t, sin_t = _rope_tables(seq)
    lower_bounds = jnp.cumsum(jax.nn.softmax(hgrn_lb_logits.astype(F32), axis=0), axis=0)
    x2d = x.reshape(n_tok, d)
    for l in range(DEPTH):
        p, logf = _in_proj(x2d.astype(BF16), w_in[l].astype(BF16), cos_t, sin_t,
                           lower_bounds[l].reshape(1, GROUP_WIDTH), seq)
        ret_o = _retention(p, ret_gn_gain[l].reshape(1, GROUP_WIDTH), batch, seq)
        hg_o = _hgrn2(p, logf, hgrn_norm_gain[l].reshape(1, GROUP_WIDTH), batch, seq)
        x1 = _out_proj(ret_o, hg_o, w_out[l].astype(BF16), x2d,
                       ln1_gain[l].reshape(1, d), ln1_bias[l].reshape(1, d))
        gate, dest, counts, seg = _router(x1, w_router[l], router_bias[l].reshape(1, N_EXPERTS))
        dest_flat = dest[:, :TOP_K].reshape(n_tok * TOP_K)
        x_sorted = _dispatch(dest_flat, seg[0], counts[0], x1)
        y_rows = _experts(seg[0], counts[0], x_sorted, w_gate[l], w_up[l], w_down[l])
        x2d = _combine(dest_flat, y_rows, x1, gate, ws_gate[l].astype(BF16),
                       ws_up[l].astype(BF16), ws_down[l].astype(BF16),
                       ln2_gain[l].reshape(1, d), ln2_bias[l].reshape(1, d))
    return x2d.reshape(batch, seq, d)
```

```python
import functools

import numpy as np
import jax
import jax.numpy as jnp
from jax import lax
from jax.experimental import pallas as pl
from jax.experimental.pallas import tpu as pltpu

D_MODEL = 2048
N_HEADS = 8
HEAD_DIM = 128
GROUP_WIDTH = N_HEADS * HEAD_DIM
N_SECTIONS = 8
ROPE_BASE = 10000.0
N_EXPERTS = 256
TOP_K = 8
N_GROUPS = 8
GROUP_SIZE = N_EXPERTS // N_GROUPS
TOPK_GROUPS = 4
EXPERT_FF = 512
SHARED_FF = 512
ROUTED_SCALE = 2.5
NORM_EPS = 1e-5
DEPTH = 1
DEEPNORM_ALPHA = (2.0 * DEPTH) ** 0.25

V7X_VMEM_LIMIT_BYTES = 56 * 1024 * 1024
LANES = 128
SUBLANES = 8

INPROJ_TM = 1024
MIX_TS = 512
RET_CHUNK = 128
HG_CHUNK = 64
HG_DIRECT = 8
OUTPROJ_TM = 256
ROUTER_TM = 256
DISPATCH_TM = 256
EXPERT_TM = 320
ROW_DMA_PRIORITY = 1
ROW_PAD = SUBLANES
COMBINE_TM = 128

F32 = jnp.float32
BF16 = jnp.bfloat16


def _sigmoid(v):
    return 1.0 / (1.0 + jnp.exp(-v))


def _silu(v):
    return v * _sigmoid(v)


def _cparams(semantics):
    return pltpu.CompilerParams(dimension_semantics=semantics,
                                vmem_limit_bytes=V7X_VMEM_LIMIT_BYTES)


def _inproj_kernel(x_ref, w_ref, cos_ref, sin_ref, lb_ref, p_ref, logf_ref):
    j = pl.program_id(1)
    acc = jnp.dot(x_ref[...], w_ref[...], preferred_element_type=F32)

    @pl.when(j < 2)
    def _rotary():
        scale = jnp.where(j == 1, HEAD_DIM ** -0.5, 1.0).astype(F32)
        cos = cos_ref[...]
        sin = sin_ref[...]
        for h in range(N_HEADS):
            t = acc[:, h * HEAD_DIM:(h + 1) * HEAD_DIM]
            r = pltpu.roll(t, HEAD_DIM // 2, axis=1)
            p_ref[:, h * HEAD_DIM:(h + 1) * HEAD_DIM] = ((t * cos + r * sin) * scale).astype(BF16)

    @pl.when(j == 5)
    def _forget():
        lb = lb_ref[...]
        f = lb + (1.0 - lb) * _sigmoid(acc)
        logf_ref[...] = jnp.log(f)
        p_ref[...] = (1.0 - f).astype(BF16)

    @pl.when(jnp.logical_and(j >= 2, j != 5))
    def _plain():
        p_ref[...] = acc.astype(BF16)


def _in_proj(x_bf, w_bf, cos_t, sin_t, lb, seq):
    n_tok = x_bf.shape[0]
    tm = INPROJ_TM
    pos_blocks = seq // tm
    return pl.pallas_call(
        _inproj_kernel,
        grid=(n_tok // tm, N_SECTIONS),
        in_specs=[
            pl.BlockSpec((tm, D_MODEL), lambda i, j: (i, 0)),
            pl.BlockSpec((D_MODEL, GROUP_WIDTH), lambda i, j: (0, j)),
            pl.BlockSpec((tm, HEAD_DIM), lambda i, j: (i % pos_blocks, 0)),
            pl.BlockSpec((tm, HEAD_DIM), lambda i, j: (i % pos_blocks, 0)),
            pl.BlockSpec((1, GROUP_WIDTH), lambda i, j: (0, 0)),
        ],
        out_specs=[
            pl.BlockSpec((tm, GROUP_WIDTH), lambda i, j: (i, j)),
            pl.BlockSpec((tm, GROUP_WIDTH), lambda i, j: (i, 0)),
        ],
        out_shape=[
            jax.ShapeDtypeStruct((n_tok, N_SECTIONS * GROUP_WIDTH), BF16),
            jax.ShapeDtypeStruct((n_tok, GROUP_WIDTH), F32),
        ],
        compiler_params=_cparams(("arbitrary", "arbitrary")),
        name="in_proj",
    )(x_bf, w_bf, cos_t, sin_t, lb)


def _ret_log_decay(h):
    return float(np.log(np.float32(1.0) - np.float32(2.0) ** np.float32(-5.0 - h)))


def _retention_kernel(q_ref, k_ref, v_ref, g_ref, gain_ref, o_ref,
                      decay_ref, qscale_ref, kscale_ref, *state_refs):
    c = RET_CHUNK

    @pl.when(pl.program_id(1) == 0)
    def _():
        for state_ref in state_refs:
            state_ref[...] = jnp.zeros_like(state_ref)

    row = lax.broadcasted_iota(jnp.int32, (c, c), 0)
    col = lax.broadcasted_iota(jnp.int32, (c, c), 1)
    causal = row >= col
    delta = jnp.where(causal, row - col, 0).astype(F32)
    rvec = lax.broadcasted_iota(jnp.int32, (c, HEAD_DIM), 0).astype(F32)
    for h in range(N_HEADS):
        ld = _ret_log_decay(h)
        decay_ref[h] = jnp.where(causal, jnp.exp(delta * ld), 0.0)
        qscale_ref[h] = jnp.exp((rvec + 1.0) * ld)
        kscale_ref[h] = jnp.exp((c - 1.0 - rvec) * ld)

    def chunk_body(ci, carry):
        rows = pl.ds(pl.multiple_of(ci * c, c), c)
        for h in range(N_HEADS):
            hs = slice(h * HEAD_DIM, (h + 1) * HEAD_DIM)
            chunk_decay = float(np.exp(np.float32(_ret_log_decay(h)) * np.float32(c)))
            q = q_ref[rows, hs]
            k = k_ref[rows, hs]
            v = v_ref[rows, hs]
            state = state_refs[h][...]
            scores = lax.dot_general(q, k, (((1,), (1,)), ((), ())),
                                     preferred_element_type=F32) * decay_ref[h]
            o = jnp.dot(scores.astype(BF16), v, preferred_element_type=F32)
            o = o + qscale_ref[h] * jnp.dot(q, state.astype(BF16), preferred_element_type=F32)
            k_dec = (k.astype(F32) * kscale_ref[h]).astype(BF16)
            kv = lax.dot_general(k_dec, v, (((0,), (0,)), ((), ())),
                                 preferred_element_type=F32)
            state_refs[h][...] = chunk_decay * state + kv
            o = o - jnp.mean(o, axis=-1, keepdims=True)
            o = o * lax.rsqrt(jnp.mean(o * o, axis=-1, keepdims=True) + NORM_EPS)
            gate = g_ref[rows, hs].astype(F32)
            o_ref[rows, hs] = (o * gain_ref[:, hs] * _silu(gate)).astype(BF16)
        return carry

    lax.fori_loop(0, MIX_TS // c, chunk_body, 0)


def _retention(p, gain, batch, seq):
    n_tok = p.shape[0]
    ts = MIX_TS
    tpb = seq // ts
    c = RET_CHUNK
    sec = lambda s: pl.BlockSpec((ts, GROUP_WIDTH), lambda b, t, s=s: (b * tpb + t, s))
    return pl.pallas_call(
        _retention_kernel,
        grid=(batch, tpb),
        in_specs=[sec(0), sec(1), sec(2), sec(3),
                  pl.BlockSpec((1, GROUP_WIDTH), lambda b, t: (0, 0))],
        out_specs=pl.BlockSpec((ts, GROUP_WIDTH), lambda b, t: (b * tpb + t, 0)),
        out_shape=jax.ShapeDtypeStruct((n_tok, GROUP_WIDTH), BF16),
        scratch_shapes=[pltpu.VMEM((N_HEADS, c, c), F32),
                        pltpu.VMEM((N_HEADS, c, HEAD_DIM), F32),
                        pltpu.VMEM((N_HEADS, c, HEAD_DIM), F32)]
                       + [pltpu.VMEM((HEAD_DIM, HEAD_DIM), F32) for _ in range(N_HEADS)],
        compiler_params=_cparams(("arbitrary", "arbitrary")),
        name="retention",
    )(p, p, p, p, gain)


def _hgrn2_kernel(q_ref, k_ref, v_ref, g_ref, logf_ref, gain_ref, o_ref, state_ref, cum_ref):
    c = HG_CHUNK
    ts = MIX_TS

    @pl.when(pl.program_id(1) == 0)
    def _():
        state_ref[...] = jnp.zeros_like(state_ref)

    trow = lax.broadcasted_iota(jnp.int32, (ts, 1), 0) % c
    cum = logf_ref[...]
    sh = 1
    while sh < c:
        cum = cum + jnp.where(trow >= sh, pltpu.roll(cum, sh, axis=0), 0.0)
        sh *= 2
    cum_ref[...] = cum

    crow = lax.broadcasted_iota(jnp.int32, (c, 1), 0)
    srow = lax.broadcasted_iota(jnp.int32, (c, c), 0)
    scol = lax.broadcasted_iota(jnp.int32, (c, c), 1)
    drow = lax.broadcasted_iota(jnp.int32, (HG_DIRECT, 1), 0)
    ones_bf = jnp.ones((HEAD_DIM, HEAD_DIM), BF16)
    n_direct = c // HG_DIRECT

    def chunk_body(ci, carry):
        rows = pl.ds(pl.multiple_of(ci * c, c), c)
        for h in range(N_HEADS):
            hs = slice(h * HEAD_DIM, (h + 1) * HEAD_DIM)
            q = q_ref[rows, hs].astype(F32)
            k = k_ref[rows, hs].astype(F32)
            v_bf = v_ref[rows, hs]
            v = v_bf.astype(F32)
            b = cum_ref[rows, hs]
            b_last = b[c - 1:c, :]
            state_t = state_ref[h]

            q_in = (q * jnp.exp(b)).astype(BF16)
            o = lax.dot_general(q_in, state_t.astype(BF16), (((1,), (1,)), ((), ())),
                                preferred_element_type=F32)

            scores = jnp.zeros((c, c), F32)
            level = 2 * HG_DIRECT
            while level <= c:
                half = level // 2
                anchors = []
                for blk in range(c // level):
                    a = blk * level + half - 1
                    anchors.append(jnp.broadcast_to(b[a:a + 1, :], (level, HEAD_DIM)))
                anchor = anchors[0] if len(anchors) == 1 else jnp.concatenate(anchors, axis=0)
                upper = (crow % level) >= half
                q_l = jnp.where(upper, q * jnp.exp(jnp.minimum(b - anchor, 0.0)), 0.0)
                k_l = jnp.where(upper, 0.0, k * jnp.exp(jnp.minimum(anchor - b, 0.0)))
                s_l = lax.dot_general(q_l.astype(BF16), k_l.astype(BF16),
                                      (((1,), (1,)), ((), ())), preferred_element_type=F32)
                same_block = (srow // level) == (scol // level)
                scores = scores + jnp.where(same_block, s_l, 0.0)
                level *= 2
            o = o + jnp.dot(scores.astype(BF16), v_bf, preferred_element_type=F32)

            prods = []
            for blk in range(n_direct):
                r0 = blk * HG_DIRECT
                qb = q[r0:r0 + HG_DIRECT, :]
                bb = b[r0:r0 + HG_DIRECT, :]
                for s in range(HG_DIRECT):
                    dec = jnp.exp(jnp.minimum(bb - b[r0 + s:r0 + s + 1, :], 0.0))
                    prods.append(jnp.where(drow >= s, qb * dec * k[r0 + s:r0 + s + 1, :], 0.0))
            prod = jnp.concatenate(prods, axis=0).astype(BF16)
            rowsum = jnp.dot(prod, ones_bf, preferred_element_type=F32)
            direct = []
            for blk in range(n_direct):
                r0 = blk * HG_DIRECT
                acc = jnp.zeros((HG_DIRECT, HEAD_DIM), F32)
                for s in range(HG_DIRECT):
                    p0 = (blk * HG_DIRECT + s) * HG_DIRECT
                    acc = acc + rowsum[p0:p0 + HG_DIRECT, :] * v[r0 + s:r0 + s + 1, :]
                direct.append(acc)
            o = o + jnp.concatenate(direct, axis=0)

            k_dec = (k * jnp.exp(b_last - b)).astype(BF16)
            vk = lax.dot_general(v_bf, k_dec, (((0,), (0,)), ((), ())),
                                 preferred_element_type=F32)
            state_ref[h] = state_t * jnp.exp(b_last) + vk

            o = o * lax.rsqrt(jnp.mean(o * o, axis=-1, keepdims=True) + NORM_EPS)
            gate = g_ref[rows, hs].astype(F32)
            o_ref[rows, hs] = (o * gain_ref[:, hs] * _silu(gate)).astype(BF16)
        return carry

    lax.fori_loop(0, ts // c, chunk_body, 0)


def _hgrn2(p, logf, gain, batch, seq):
    n_tok = p.shape[0]
    ts = MIX_TS
    tpb = seq // ts
    sec = lambda s: pl.BlockSpec((ts, GROUP_WIDTH), lambda b, t, s=s: (b * tpb + t, s))
    return pl.pallas_call(
        _hgrn2_kernel,
        grid=(batch, tpb),
        in_specs=[sec(4), sec(5), sec(6), sec(7),
                  pl.BlockSpec((ts, GROUP_WIDTH), lambda b, t: (b * tpb + t, 0)),
                  pl.BlockSpec((1, GROUP_WIDTH), lambda b, t: (0, 0))],
        out_specs=pl.BlockSpec((ts, GROUP_WIDTH), lambda b, t: (b * tpb + t, 0)),
        out_shape=jax.ShapeDtypeStruct((n_tok, GROUP_WIDTH), BF16),
        scratch_shapes=[pltpu.VMEM((N_HEADS, HEAD_DIM, HEAD_DIM), F32),
                        pltpu.VMEM((ts, GROUP_WIDTH), F32)],
        compiler_params=_cparams(("arbitrary", "arbitrary")),
        name="hgrn2",
    )(p, p, p, p, logf, gain)


def _layer_norm(z, gain, bias):
    mu = jnp.mean(z, axis=-1, keepdims=True)
    zc = z - mu
    var = jnp.mean(zc * zc, axis=-1, keepdims=True)
    return zc * lax.rsqrt(var + NORM_EPS) * gain + bias


def _outproj_kernel(ret_ref, hg_ref, w_ref, x_ref, gain_ref, bias_ref, o_ref):
    mix = jnp.dot(ret_ref[...], w_ref[0:GROUP_WIDTH, :], preferred_element_type=F32)
    mix = mix + jnp.dot(hg_ref[...], w_ref[GROUP_WIDTH:2 * GROUP_WIDTH, :],
                        preferred_element_type=F32)
    z = DEEPNORM_ALPHA * x_ref[...] + mix
    o_ref[...] = _layer_norm(z, gain_ref[...], bias_ref[...])


def _out_proj(ret_o, hg_o, w_bf, x2d, gain, bias):
    n_tok = x2d.shape[0]
    tm = OUTPROJ_TM
    row = lambda w: pl.BlockSpec((tm, w), lambda i: (i, 0))
    full = lambda r, w: pl.BlockSpec((r, w), lambda i: (0, 0))
    return pl.pallas_call(
        _outproj_kernel,
        grid=(n_tok // tm,),
        in_specs=[row(GROUP_WIDTH), row(GROUP_WIDTH), full(2 * GROUP_WIDTH, D_MODEL),
                  row(D_MODEL), full(1, D_MODEL), full(1, D_MODEL)],
        out_specs=row(D_MODEL),
        out_shape=jax.ShapeDtypeStruct((n_tok, D_MODEL), F32),
        compiler_params=_cparams(("arbitrary",)),
        name="out_proj",
    )(ret_o, hg_o, w_bf, x2d, gain, bias)


def _router_kernel(x_ref, w_ref, bias_ref, gate_ref, dest_ref, count_ref, seg_ref,
                   base_ref, eidx_scr, rank_scr):
    tm = ROUTER_TM
    step = pl.program_id(0)
    tok_cols = pl.ds(pl.multiple_of(step * tm, tm), tm)

    @pl.when(step == 0)
    def _():
        base_ref[...] = jnp.zeros_like(base_ref)

    logits = jnp.dot(x_ref[...], w_ref[...], preferred_element_type=F32,
                     precision=lax.Precision.HIGHEST)
    scores = _sigmoid(logits.T)
    sel = scores + bias_ref[...]
    neg = -jnp.inf
    erow = lax.broadcasted_iota(jnp.int32, (N_EXPERTS, tm), 0).astype(F32)
    grow = lax.broadcasted_iota(jnp.int32, (GROUP_SIZE, tm), 0).astype(F32)

    def first_argmax(vals, rows, sentinel):
        m = jnp.max(vals, axis=0, keepdims=True)
        idx = jnp.min(jnp.where(vals == m, rows, sentinel), axis=0, keepdims=True)
        return m, idx

    group_vals, group_score = [], []
    for g in range(N_GROUPS):
        vals = sel[g * GROUP_SIZE:(g + 1) * GROUP_SIZE, :]
        m1, i1 = first_argmax(vals, grow, float(GROUP_SIZE))
        m2 = jnp.max(jnp.where(grow == i1, neg, vals), axis=0, keepdims=True)
        group_vals.append(vals)
        group_score.append(m1 + m2)

    kept = []
    for g in range(N_GROUPS):
        beaten_by = jnp.zeros((1, tm), F32)
        for o in range(N_GROUPS):
            if o == g:
                continue
            wins = (group_score[o] >= group_score[g]) if o < g else (group_score[o] > group_score[g])
            beaten_by = beaten_by + jnp.where(wins, 1.0, 0.0)
        kept.append(jnp.where(beaten_by < TOPK_GROUPS, group_vals[g], neg))
    cur = jnp.concatenate(kept, axis=0)

    idx_rows, gate_rows, onehots = [], [], []
    for kk in range(TOP_K):
        _, idx = first_argmax(cur, erow, float(N_EXPERTS))
        onehot = erow == idx
        gate_rows.append(jnp.sum(jnp.where(onehot, scores, 0.0), axis=0, keepdims=True))
        cur = jnp.where(onehot, neg, cur)
        onehots.append(onehot)
        idx_rows.append(idx)
    gate_t = jnp.concatenate(gate_rows, axis=0)
    gate_ref[...] = gate_t / jnp.sum(gate_t, axis=0, keepdims=True) * ROUTED_SCALE
    eidx_scr[:, tok_cols] = jnp.concatenate(idx_rows, axis=0)

    multihot = jnp.zeros((N_EXPERTS, tm), F32)
    for onehot in onehots:
        multihot = multihot + jnp.where(onehot, 1.0, 0.0)
    trow = lax.broadcasted_iota(jnp.int32, (tm, tm), 0)
    tcol = lax.broadcasted_iota(jnp.int32, (tm, tm), 1)
    earlier = jnp.where(trow < tcol, 1.0, 0.0).astype(BF16)
    before = jnp.dot(multihot.astype(BF16), earlier, preferred_element_type=F32)
    pos = before + base_ref[...]
    rank_rows = [jnp.sum(jnp.where(onehot, pos, 0.0), axis=0, keepdims=True) for onehot in onehots]
    rank_scr[:, tok_cols] = jnp.concatenate(rank_rows, axis=0)
    new_base = base_ref[...] + jnp.sum(multihot, axis=1, keepdims=True)
    base_ref[...] = new_base

    @pl.when(step == pl.num_programs(0) - 1)
    def _():
        counts = new_base
        padded = jnp.floor((counts + (ROW_PAD - 1)) / ROW_PAD) * ROW_PAD
        hi = jnp.floor(padded / 256.0)
        lo = padded - 256.0 * hi
        prow = lax.broadcasted_iota(jnp.int32, (N_EXPERTS, N_EXPERTS), 0)
        pcol = lax.broadcasted_iota(jnp.int32, (N_EXPERTS, N_EXPERTS), 1)
        before_e = jnp.where(pcol < prow, 1.0, 0.0).astype(BF16)
        bcast = lambda v: jnp.broadcast_to(v, (N_EXPERTS, LANES)).astype(BF16)
        seg = (256.0 * jnp.dot(before_e, bcast(hi), preferred_element_type=F32)
               + jnp.dot(before_e, bcast(lo), preferred_element_type=F32))[:, 0:1]
        count_ref[...] = counts.astype(jnp.int32)
        seg_ref[...] = seg.astype(jnp.int32)

        def tile_body(t, carry):
            cols = pl.ds(pl.multiple_of(t * tm, tm), tm)
            eidx = eidx_scr[:, cols]
            starts = [jnp.sum(jnp.where(erow == eidx[kk:kk + 1, :], seg, 0.0), axis=0, keepdims=True)
                      for kk in range(TOP_K)]
            dest = jnp.concatenate(starts, axis=0) + rank_scr[:, cols]
            dest_ref[:, cols] = dest.astype(jnp.int32)
            return carry

        lax.fori_loop(0, pl.num_programs(0), tile_body, 0)


def _router(x1, w_router, bias):
    n_tok = x1.shape[0]
    tm = ROUTER_TM
    row = lambda w: pl.BlockSpec((tm, w), lambda i: (i, 0))
    full = lambda r, w: pl.BlockSpec((r, w), lambda i: (0, 0))
    return pl.pallas_call(
        _router_kernel,
        grid=(n_tok // tm,),
        in_specs=[row(D_MODEL), full(D_MODEL, N_EXPERTS), full(N_EXPERTS, 1)],
        out_specs=[pl.BlockSpec((TOP_K, tm), lambda i: (0, i)), full(TOP_K, n_tok),
                   full(N_EXPERTS, 1), full(N_EXPERTS, 1)],
        out_shape=[jax.ShapeDtypeStruct((TOP_K, n_tok), F32),
                   jax.ShapeDtypeStruct((TOP_K, n_tok), jnp.int32),
                   jax.ShapeDtypeStruct((N_EXPERTS, 1), jnp.int32),
                   jax.ShapeDtypeStruct((N_EXPERTS, 1), jnp.int32)],
        scratch_shapes=[pltpu.VMEM((N_EXPERTS, 1), F32),
                        pltpu.VMEM((TOP_K, n_tok), F32),
                        pltpu.VMEM((TOP_K, n_tok), F32)],
        compiler_params=_cparams(("arbitrary",)),
        name="router",
    )(x1, w_router, bias)


def _sorted_rows(n_tok):
    return n_tok * TOP_K + N_EXPERTS * ROW_PAD + EXPERT_TM


def _padded_count(count):
    return (count + (ROW_PAD - 1)) // ROW_PAD * ROW_PAD


def _dispatch_row_copy(x_ref, xs_hbm, sem, row, dst_row):
    return pltpu.make_async_copy(x_ref.at[pl.ds(row, 1), :], xs_hbm.at[pl.ds(dst_row, 1), :], sem)


def _dispatch_kernel(dest_ref, seg_ref, count_ref, x_ref, xs_hbm, zero_buf, sem, zsem):
    tm = DISPATCH_TM
    i = pl.program_id(0)
    group = SUBLANES

    @pl.when(i == 0)
    def _():
        zero_buf[...] = jnp.zeros_like(zero_buf)

        def pad_copy(e, j):
            row = seg_ref[e] + count_ref[e] + j
            return pltpu.make_async_copy(zero_buf.at[pl.ds(0, 1), :], xs_hbm.at[pl.ds(row, 1), :], zsem)

        def pad_body(e, carry):
            n_pad = _padded_count(count_ref[e]) - count_ref[e]
            for j in range(ROW_PAD - 1):
                @pl.when(j < n_pad)
                def _(j=j):
                    pad_copy(e, j).start()
            for j in range(ROW_PAD - 1):
                @pl.when(j < n_pad)
                def _(j=j):
                    pad_copy(e, j).wait()
            return carry

        lax.fori_loop(0, N_EXPERTS, pad_body, 0)
        last = N_EXPERTS - 1
        n_rows = xs_hbm.shape[0]
        total = seg_ref[last] + _padded_count(count_ref[last])

        def tail_body(j, carry):
            row0 = pl.multiple_of(jnp.minimum(total + j * EXPERT_TM, n_rows - EXPERT_TM), ROW_PAD)
            tail = pltpu.make_async_copy(zero_buf, xs_hbm.at[pl.ds(row0, EXPERT_TM), :], zsem)
            tail.start()
            tail.wait()
            return carry

        lax.fori_loop(0, (n_rows - total + EXPERT_TM - 1) // EXPERT_TM, tail_body, 0)

    n_tok = pl.num_programs(0) * tm

    def issue_body(g, carry):
        for rr in range(group):
            row = g * group + rr
            tok = i * tm + row
            for kk in range(TOP_K):
                _dispatch_row_copy(x_ref, xs_hbm, sem, row, dest_ref[kk * n_tok + tok]).start()
        return carry

    lax.fori_loop(0, tm // group, issue_body, 0)

    def wait_body(g, carry):
        for _ in range(group * TOP_K):
            _dispatch_row_copy(x_ref, xs_hbm, sem, 0, 0).wait()
        return carry

    lax.fori_loop(0, tm // group, wait_body, 0)


def _dispatch(dest_flat, seg, counts, x1):
    n_tok = x1.shape[0]
    tm = DISPATCH_TM
    grid_spec = pltpu.PrefetchScalarGridSpec(
        num_scalar_prefetch=3,
        grid=(n_tok // tm,),
        in_specs=[pl.BlockSpec((tm, D_MODEL), lambda i, d, s, c: (i, 0))],
        out_specs=pl.BlockSpec(memory_space=pl.ANY),
        scratch_shapes=[pltpu.VMEM((EXPERT_TM, D_MODEL), F32),
                        pltpu.SemaphoreType.DMA(()),
                        pltpu.SemaphoreType.DMA(())],
    )
    return pl.pallas_call(
        _dispatch_kernel,
        grid_spec=grid_spec,
        out_shape=jax.ShapeDtypeStruct((_sorted_rows(n_tok), D_MODEL), F32),
        compiler_params=_cparams(("arbitrary",)),
        name="dispatch",
    )(dest_flat, seg, counts, x1)


def _experts_kernel(seg_ref, count_ref, xs_hbm, wg_ref, wu_ref, wd_ref, y_hbm,
                    xbuf, ybuf, in_sem, out_sem, pending_ref):
    tm = EXPERT_TM
    e = pl.program_id(0)
    n_experts = pl.num_programs(0)
    n_rows = y_hbm.shape[0]
    count = count_ref[e]
    seg = seg_ref[e]
    n_chunks = (count + tm - 1) // tm

    def x_copy(row0, slot):
        return pltpu.make_async_copy(xs_hbm.at[pl.ds(pl.multiple_of(row0, ROW_PAD), tm), :],
                                     xbuf.at[slot], in_sem.at[slot])

    def y_copy(row0, slot):
        return pltpu.make_async_copy(ybuf.at[slot],
                                     y_hbm.at[pl.ds(pl.multiple_of(row0, ROW_PAD), tm), :], out_sem)

    def drain_output():
        @pl.when(pending_ref[0] == 1)
        def _():
            y_copy(0, 0).wait()
            pending_ref[0] = 0

    def prefetch_first_chunks(expert):
        first = seg_ref[expert]
        n_first = count_ref[expert]

        @pl.when(n_first > 0)
        def _():
            x_copy(first, 0).start(priority=ROW_DMA_PRIORITY)

        @pl.when(n_first > tm)
        def _():
            x_copy(first + tm, 1).start(priority=ROW_DMA_PRIORITY)

    @pl.when(e == 0)
    def _():
        pending_ref[0] = 0
        pending_ref[1] = 0
        prefetch_first_chunks(0)

    def chunk_body(c, carry):
        slot = c % 2

        @pl.when(jnp.logical_and(c >= 1, c + 1 < n_chunks))
        def _():
            x_copy(seg + (c + 1) * tm, 1 - slot).start(priority=ROW_DMA_PRIORITY)

        x_copy(0, slot).wait()
        x = xbuf[slot].astype(BF16)
        g = jnp.dot(x, wg_ref[...].astype(BF16), preferred_element_type=F32)
        u = jnp.dot(x, wu_ref[...].astype(BF16), preferred_element_type=F32)
        hmid = (_silu(g) * u).astype(BF16)
        yslot = pending_ref[1]
        ybuf[yslot] = jnp.dot(hmid, wd_ref[...].astype(BF16), preferred_element_type=F32)
        drain_output()
        y_copy(seg + c * tm, yslot).start(priority=ROW_DMA_PRIORITY)
        pending_ref[0] = 1
        pending_ref[1] = 1 - yslot
        return carry

    lax.fori_loop(0, n_chunks, chunk_body, 0)

    @pl.when(e + 1 < n_experts)
    def _():
        prefetch_first_chunks(jnp.minimum(e + 1, n_experts - 1))

    @pl.when(e == n_experts - 1)
    def _():
        drain_output()
        total = seg + _padded_count(count)
        ybuf[0] = jnp.zeros((tm, D_MODEL), F32)
        n_fill = (n_rows - total + tm - 1) // tm

        def fill_body(j, carry):
            fill = y_copy(jnp.minimum(total + j * tm, n_rows - tm), 0)
            fill.start()
            fill.wait()
            return carry

        lax.fori_loop(0, n_fill, fill_body, 0)


def _experts(seg, counts, xs, w_gate, w_up, w_down):
    n_rows = xs.shape[0]
    tm = EXPERT_TM
    wspec = lambda r, c: pl.BlockSpec((None, r, c), lambda e, s, n: (e, 0, 0))
    grid_spec = pltpu.PrefetchScalarGridSpec(
        num_scalar_prefetch=2,
        grid=(N_EXPERTS,),
        in_specs=[pl.BlockSpec(memory_space=pl.ANY),
                  wspec(D_MODEL, EXPERT_FF), wspec(D_MODEL, EXPERT_FF), wspec(EXPERT_FF, D_MODEL)],
        out_specs=pl.BlockSpec(memory_space=pl.ANY),
        scratch_shapes=[pltpu.VMEM((2, tm, D_MODEL), F32),
                        pltpu.VMEM((2, tm, D_MODEL), F32),
                        pltpu.SemaphoreType.DMA((2,)),
                        pltpu.SemaphoreType.DMA(()),
                        pltpu.SMEM((2,), jnp.int32)],
    )
    return pl.pallas_call(
        _experts_kernel,
        grid_spec=grid_spec,
        out_shape=jax.ShapeDtypeStruct((n_rows, D_MODEL), F32),
        compiler_params=_cparams(("arbitrary",)),
        name="experts",
    )(seg, counts, xs, w_gate, w_up, w_down)


def _y_copy(y_hbm, ybuf, sem, src_row, slot, kk, r):
    return pltpu.make_async_copy(y_hbm.at[pl.ds(src_row, 1), :],
                                 ybuf.at[slot, kk, pl.ds(r, 1), :], sem.at[slot])


def _combine_kernel(dest_ref, y_hbm, x_ref, gate_ref, wsg_ref, wsu_ref, wsd_ref,
                    gain_ref, bias_ref, o_ref, ybuf, sem):
    tm = COMBINE_TM
    i = pl.program_id(0)
    n_tiles = pl.num_programs(0)
    slot = i % 2

    group = SUBLANES
    n_tok = n_tiles * tm

    def issue(tile, dst_slot):
        def body(g, carry):
            for rr in range(group):
                r = g * group + rr
                tok = tile * tm + r
                for kk in range(TOP_K):
                    _y_copy(y_hbm, ybuf, sem, dest_ref[kk * n_tok + tok], dst_slot, kk, r).start()
            return carry
        lax.fori_loop(0, tm // group, body, 0)

    def wait_all(dst_slot):
        def body(g, carry):
            for _ in range(group * TOP_K):
                _y_copy(y_hbm, ybuf, sem, 0, dst_slot, 0, 0).wait()
            return carry
        lax.fori_loop(0, tm // group, body, 0)

    @pl.when(i == 0)
    def _():
        issue(0, 0)

    @pl.when(i + 1 < n_tiles)
    def _():
        issue(i + 1, 1 - slot)

    x = x_ref[...]
    xb = x.astype(BF16)
    g = jnp.dot(xb, wsg_ref[...], preferred_element_type=F32)
    u = jnp.dot(xb, wsu_ref[...], preferred_element_type=F32)
    shared = jnp.dot((_silu(g) * u).astype(BF16), wsd_ref[...], preferred_element_type=F32)

    wait_all(slot)
    gate_t = gate_ref[...]
    diag = (lax.broadcasted_iota(jnp.int32, (tm, tm), 0)
            == lax.broadcasted_iota(jnp.int32, (tm, tm), 1))
    routed = jnp.zeros((tm, D_MODEL), F32)
    for kk in range(TOP_K):
        gate_col = jnp.sum(jnp.where(diag, gate_t[kk:kk + 1, :], 0.0), axis=1, keepdims=True)
        routed = routed + gate_col * ybuf[slot, kk]
    z = DEEPNORM_ALPHA * x + (routed + shared)
    o_ref[...] = _layer_norm(z, gain_ref[...], bias_ref[...])


def _combine(dest_flat, y_rows, x1, gate, wsg, wsu, wsd, gain, bias):
    n_tok = x1.shape[0]
    tm = COMBINE_TM
    row = lambda w: pl.BlockSpec((tm, w), lambda i, d: (i, 0))
    full = lambda r, w: pl.BlockSpec((r, w), lambda i, d: (0, 0))
    grid_spec = pltpu.PrefetchScalarGridSpec(
        num_scalar_prefetch=1,
        grid=(n_tok // tm,),
        in_specs=[pl.BlockSpec(memory_space=pl.ANY), row(D_MODEL),
                  pl.BlockSpec((TOP_K, tm), lambda i, d: (0, i)),
                  full(D_MODEL, SHARED_FF), full(D_MODEL, SHARED_FF), full(SHARED_FF, D_MODEL),
                  full(1, D_MODEL), full(1, D_MODEL)],
        out_specs=row(D_MODEL),
        scratch_shapes=[pltpu.VMEM((2, TOP_K, tm, D_MODEL), F32),
                        pltpu.SemaphoreType.DMA((2,))],
    )
    return pl.pallas_call(
        _combine_kernel,
        grid_spec=grid_spec,
        out_shape=jax.ShapeDtypeStruct((n_tok, D_MODEL), F32),
        compiler_params=_cparams(("arbitrary",)),
        name="combine",
    )(dest_flat, y_rows, x1, gate, wsg, wsu, wsd, gain, bias)


def _rope_tables(seq):
    half = HEAD_DIM // 2
    pos = jnp.arange(seq, dtype=F32)
    inv_freq = ROPE_BASE ** (-jnp.arange(half, dtype=F32) / half)
    ang = pos[:, None] * inv_freq[None, :]
    cos, sin = jnp.cos(ang), jnp.sin(ang)
    return jnp.concatenate([cos, cos], axis=-1), jnp.concatenate([-sin, sin], axis=-1)


def kernel(x, w_in, ret_gn_gain, hgrn_lb_logits, hgrn_norm_gain, w_out, ln1_gain, ln1_bias,
           w_router, router_bias, w_gate, w_up, w_down, ws_gate, ws_up, ws_down,
           ln2_gain, ln2_bias):
    batch, seq, d = x.shape
    n_tok = batch * seq
    cos_t, sin_t = _rope_tables(seq)
    lower_bounds = jnp.cumsum(jax.nn.softmax(hgrn_lb_logits.astype(F32), axis=0), axis=0)
    x2d = x.reshape(n_tok, d)
    for l in range(DEPTH):
        p, logf = _in_proj(x2d.astype(BF16), w_in[l].astype(BF16), cos_t, sin_t,
                           lower_bounds[l].reshape(1, GROUP_WIDTH), seq)
        ret_o = _retention(p, ret_gn_gain[l].reshape(1, GROUP_WIDTH), batch, seq)
        hg_o = _hgrn2(p, logf, hgrn_norm_gain[l].reshape(1, GROUP_WIDTH), batch, seq)
        x1 = _out_proj(ret_o, hg_o, w_out[l].astype(BF16), x2d,
                       ln1_gain[l].reshape(1, d), ln1_bias[l].reshape(1, d))
        gate, dest, counts, seg = _router(x1, w_router[l], router_bias[l].reshape(N_EXPERTS, 1))
        dest_flat = dest.reshape(TOP_K * n_tok)
        counts, seg = counts.reshape(N_EXPERTS), seg.reshape(N_EXPERTS)
        x_sorted = _dispatch(dest_flat, seg, counts, x1)
        y_rows = _experts(seg, counts, x_sorted, w_gate[l], w_up[l], w_down[l])
        x2d = _combine(dest_flat, y_rows, x1, gate, ws_gate[l].astype(BF16),
                       ws_up[l].astype(BF16), ws_down[l].astype(BF16),
                       ln2_gain[l].reshape(1, d), ln2_bias[l].reshape(1, d))
    return x2d.reshape(batch, seq, d)
```

```python
import functools

import numpy as np
import jax
import jax.numpy as jnp
from jax import lax
from jax.experimental import pallas as pl
from jax.experimental.pallas import tpu as pltpu

D_MODEL = 2048
N_HEADS = 8
HEAD_DIM = 128
GROUP_WIDTH = N_HEADS * HEAD_DIM
N_SECTIONS = 8
ROPE_BASE = 10000.0
N_EXPERTS = 256
TOP_K = 8
N_GROUPS = 8
GROUP_SIZE = N_EXPERTS // N_GROUPS
TOPK_GROUPS = 4
EXPERT_FF = 512
SHARED_FF = 512
ROUTED_SCALE = 2.5
NORM_EPS = 1e-5
DEPTH = 1
DEEPNORM_ALPHA = (2.0 * DEPTH) ** 0.25

V7X_VMEM_LIMIT_BYTES = 56 * 1024 * 1024
LANES = 128
SUBLANES = 8

INPROJ_TM = 1024
MIX_TS = 512
RET_CHUNK = 128
HG_CHUNK = 64
HG_DIRECT = 8
OUTPROJ_TM = 256
ROUTER_TM = 256
DISPATCH_TM = 256
EXPERT_TM = 320
WEIGHT_DMA_PRIORITY = 1
N_DMA_QUEUES = 2
ROW_PAD = SUBLANES
COMBINE_TM = 128

F32 = jnp.float32
BF16 = jnp.bfloat16


def _sigmoid(v):
    return 1.0 / (1.0 + jnp.exp(-v))


def _silu(v):
    return v * _sigmoid(v)


def _cparams(semantics):
    return pltpu.CompilerParams(dimension_semantics=semantics,
                                vmem_limit_bytes=V7X_VMEM_LIMIT_BYTES)


def _inproj_kernel(x_ref, w_ref, cos_ref, sin_ref, lb_ref, p_ref, logf_ref):
    j = pl.program_id(1)
    acc = jnp.dot(x_ref[...], w_ref[...], preferred_element_type=F32)

    @pl.when(j < 2)
    def _rotary():
        scale = jnp.where(j == 1, HEAD_DIM ** -0.5, 1.0).astype(F32)
        cos = cos_ref[...]
        sin = sin_ref[...]
        for h in range(N_HEADS):
            t = acc[:, h * HEAD_DIM:(h + 1) * HEAD_DIM]
            r = pltpu.roll(t, HEAD_DIM // 2, axis=1)
            p_ref[:, h * HEAD_DIM:(h + 1) * HEAD_DIM] = ((t * cos + r * sin) * scale).astype(BF16)

    @pl.when(j == 5)
    def _forget():
        lb = lb_ref[...]
        f = lb + (1.0 - lb) * _sigmoid(acc)
        logf_ref[...] = jnp.log(f)
        p_ref[...] = (1.0 - f).astype(BF16)

    @pl.when(jnp.logical_and(j >= 2, j != 5))
    def _plain():
        p_ref[...] = acc.astype(BF16)


def _in_proj(x_bf, w_bf, cos_t, sin_t, lb, seq):
    n_tok = x_bf.shape[0]
    tm = INPROJ_TM
    pos_blocks = seq // tm
    return pl.pallas_call(
        _inproj_kernel,
        grid=(n_tok // tm, N_SECTIONS),
        in_specs=[
            pl.BlockSpec((tm, D_MODEL), lambda i, j: (i, 0)),
            pl.BlockSpec((D_MODEL, GROUP_WIDTH), lambda i, j: (0, j)),
            pl.BlockSpec((tm, HEAD_DIM), lambda i, j: (i % pos_blocks, 0)),
            pl.BlockSpec((tm, HEAD_DIM), lambda i, j: (i % pos_blocks, 0)),
            pl.BlockSpec((1, GROUP_WIDTH), lambda i, j: (0, 0)),
        ],
        out_specs=[
            pl.BlockSpec((tm, GROUP_WIDTH), lambda i, j: (i, j)),
            pl.BlockSpec((tm, GROUP_WIDTH), lambda i, j: (i, 0)),
        ],
        out_shape=[
            jax.ShapeDtypeStruct((n_tok, N_SECTIONS * GROUP_WIDTH), BF16),
            jax.ShapeDtypeStruct((n_tok, GROUP_WIDTH), F32),
        ],
        compiler_params=_cparams(("arbitrary", "arbitrary")),
        name="in_proj",
    )(x_bf, w_bf, cos_t, sin_t, lb)


def _ret_log_decay(h):
    return float(np.log(np.float32(1.0) - np.float32(2.0) ** np.float32(-5.0 - h)))


def _retention_kernel(q_ref, k_ref, v_ref, g_ref, gain_ref, o_ref,
                      decay_ref, qscale_ref, kscale_ref, *state_refs):
    c = RET_CHUNK

    @pl.when(pl.program_id(1) == 0)
    def _():
        for state_ref in state_refs:
            state_ref[...] = jnp.zeros_like(state_ref)

    row = lax.broadcasted_iota(jnp.int32, (c, c), 0)
    col = lax.broadcasted_iota(jnp.int32, (c, c), 1)
    causal = row >= col
    delta = jnp.where(causal, row - col, 0).astype(F32)
    rvec = lax.broadcasted_iota(jnp.int32, (c, HEAD_DIM), 0).astype(F32)
    for h in range(N_HEADS):
        ld = _ret_log_decay(h)
        decay_ref[h] = jnp.where(causal, jnp.exp(delta * ld), 0.0)
        qscale_ref[h] = jnp.exp((rvec + 1.0) * ld)
        kscale_ref[h] = jnp.exp((c - 1.0 - rvec) * ld)

    def chunk_body(ci, carry):
        rows = pl.ds(pl.multiple_of(ci * c, c), c)
        for h in range(N_HEADS):
            hs = slice(h * HEAD_DIM, (h + 1) * HEAD_DIM)
            chunk_decay = float(np.exp(np.float32(_ret_log_decay(h)) * np.float32(c)))
            q = q_ref[rows, hs]
            k = k_ref[rows, hs]
            v = v_ref[rows, hs]
            state = state_refs[h][...]
            scores = lax.dot_general(q, k, (((1,), (1,)), ((), ())),
                                     preferred_element_type=F32) * decay_ref[h]
            o = jnp.dot(scores.astype(BF16), v, preferred_element_type=F32)
            o = o + qscale_ref[h] * jnp.dot(q, state.astype(BF16), preferred_element_type=F32)
            k_dec = (k.astype(F32) * kscale_ref[h]).astype(BF16)
            kv = lax.dot_general(k_dec, v, (((0,), (0,)), ((), ())),
                                 preferred_element_type=F32)
            state_refs[h][...] = chunk_decay * state + kv
            o = o - jnp.mean(o, axis=-1, keepdims=True)
            o = o * lax.rsqrt(jnp.mean(o * o, axis=-1, keepdims=True) + NORM_EPS)
            gate = g_ref[rows, hs].astype(F32)
            o_ref[rows, hs] = (o * gain_ref[:, hs] * _silu(gate)).astype(BF16)
        return carry

    lax.fori_loop(0, MIX_TS // c, chunk_body, 0)


def _retention(p, gain, batch, seq):
    n_tok = p.shape[0]
    ts = MIX_TS
    tpb = seq // ts
    c = RET_CHUNK
    sec = lambda s: pl.BlockSpec((ts, GROUP_WIDTH), lambda b, t, s=s: (b * tpb + t, s))
    return pl.pallas_call(
        _retention_kernel,
        grid=(batch, tpb),
        in_specs=[sec(0), sec(1), sec(2), sec(3),
                  pl.BlockSpec((1, GROUP_WIDTH), lambda b, t: (0, 0))],
        out_specs=pl.BlockSpec((ts, GROUP_WIDTH), lambda b, t: (b * tpb + t, 0)),
        out_shape=jax.ShapeDtypeStruct((n_tok, GROUP_WIDTH), BF16),
        scratch_shapes=[pltpu.VMEM((N_HEADS, c, c), F32),
                        pltpu.VMEM((N_HEADS, c, HEAD_DIM), F32),
                        pltpu.VMEM((N_HEADS, c, HEAD_DIM), F32)]
                       + [pltpu.VMEM((HEAD_DIM, HEAD_DIM), F32) for _ in range(N_HEADS)],
        compiler_params=_cparams(("arbitrary", "arbitrary")),
        name="retention",
    )(p, p, p, p, gain)


def _hgrn2_kernel(q_ref, k_ref, v_ref, g_ref, logf_ref, gain_ref, o_ref, state_ref, cum_ref):
    c = HG_CHUNK
    ts = MIX_TS

    @pl.when(pl.program_id(1) == 0)
    def _():
        state_ref[...] = jnp.zeros_like(state_ref)

    trow = lax.broadcasted_iota(jnp.int32, (ts, 1), 0) % c
    cum = logf_ref[...]
    sh = 1
    while sh < c:
        cum = cum + jnp.where(trow >= sh, pltpu.roll(cum, sh, axis=0), 0.0)
        sh *= 2
    cum_ref[...] = cum

    crow = lax.broadcasted_iota(jnp.int32, (c, 1), 0)
    srow = lax.broadcasted_iota(jnp.int32, (c, c), 0)
    scol = lax.broadcasted_iota(jnp.int32, (c, c), 1)
    drow = lax.broadcasted_iota(jnp.int32, (HG_DIRECT, 1), 0)
    ones_bf = jnp.ones((HEAD_DIM, HEAD_DIM), BF16)
    n_direct = c // HG_DIRECT

    def chunk_body(ci, carry):
        rows = pl.ds(pl.multiple_of(ci * c, c), c)
        for h in range(N_HEADS):
            hs = slice(h * HEAD_DIM, (h + 1) * HEAD_DIM)
            q = q_ref[rows, hs].astype(F32)
            k = k_ref[rows, hs].astype(F32)
            v_bf = v_ref[rows, hs]
            v = v_bf.astype(F32)
            b = cum_ref[rows, hs]
            b_last = b[c - 1:c, :]
            state_t = state_ref[h]

            q_in = (q * jnp.exp(b)).astype(BF16)
            o = lax.dot_general(q_in, state_t.astype(BF16), (((1,), (1,)), ((), ())),
                                preferred_element_type=F32)

            scores = jnp.zeros((c, c), F32)
            level = 2 * HG_DIRECT
            while level <= c:
                half = level // 2
                anchors = []
                for blk in range(c // level):
                    a = blk * level + half - 1
                    anchors.append(jnp.broadcast_to(b[a:a + 1, :], (level, HEAD_DIM)))
                anchor = anchors[0] if len(anchors) == 1 else jnp.concatenate(anchors, axis=0)
                upper = (crow % level) >= half
                q_l = jnp.where(upper, q * jnp.exp(jnp.minimum(b - anchor, 0.0)), 0.0)
                k_l = jnp.where(upper, 0.0, k * jnp.exp(jnp.minimum(anchor - b, 0.0)))
                s_l = lax.dot_general(q_l.astype(BF16), k_l.astype(BF16),
                                      (((1,), (1,)), ((), ())), preferred_element_type=F32)
                same_block = (srow // level) == (scol // level)
                scores = scores + jnp.where(same_block, s_l, 0.0)
                level *= 2
            o = o + jnp.dot(scores.astype(BF16), v_bf, preferred_element_type=F32)

            prods = []
            for blk in range(n_direct):
                r0 = blk * HG_DIRECT
                qb = q[r0:r0 + HG_DIRECT, :]
                bb = b[r0:r0 + HG_DIRECT, :]
                for s in range(HG_DIRECT):
                    dec = jnp.exp(jnp.minimum(bb - b[r0 + s:r0 + s + 1, :], 0.0))
                    prods.append(jnp.where(drow >= s, qb * dec * k[r0 + s:r0 + s + 1, :], 0.0))
            prod = jnp.concatenate(prods, axis=0).astype(BF16)
            rowsum = jnp.dot(prod, ones_bf, preferred_element_type=F32)
            direct = []
            for blk in range(n_direct):
                r0 = blk * HG_DIRECT
                acc = jnp.zeros((HG_DIRECT, HEAD_DIM), F32)
                for s in range(HG_DIRECT):
                    p0 = (blk * HG_DIRECT + s) * HG_DIRECT
                    acc = acc + rowsum[p0:p0 + HG_DIRECT, :] * v[r0 + s:r0 + s + 1, :]
                direct.append(acc)
            o = o + jnp.concatenate(direct, axis=0)

            k_dec = (k * jnp.exp(b_last - b)).astype(BF16)
            vk = lax.dot_general(v_bf, k_dec, (((0,), (0,)), ((), ())),
                                 preferred_element_type=F32)
            state_ref[h] = state_t * jnp.exp(b_last) + vk

            o = o * lax.rsqrt(jnp.mean(o * o, axis=-1, keepdims=True) + NORM_EPS)
            gate = g_ref[rows, hs].astype(F32)
            o_ref[rows, hs] = (o * gain_ref[:, hs] * _silu(gate)).astype(BF16)
        return carry

    lax.fori_loop(0, ts // c, chunk_body, 0)


def _hgrn2(p, logf, gain, batch, seq):
    n_tok = p.shape[0]
    ts = MIX_TS
    tpb = seq // ts
    sec = lambda s: pl.BlockSpec((ts, GROUP_WIDTH), lambda b, t, s=s: (b * tpb + t, s))
    return pl.pallas_call(
        _hgrn2_kernel,
        grid=(batch, tpb),
        in_specs=[sec(4), sec(5), sec(6), sec(7),
                  pl.BlockSpec((ts, GROUP_WIDTH), lambda b, t: (b * tpb + t, 0)),
                  pl.BlockSpec((1, GROUP_WIDTH), lambda b, t: (0, 0))],
        out_specs=pl.BlockSpec((ts, GROUP_WIDTH), lambda b, t: (b * tpb + t, 0)),
        out_shape=jax.ShapeDtypeStruct((n_tok, GROUP_WIDTH), BF16),
        scratch_shapes=[pltpu.VMEM((N_HEADS, HEAD_DIM, HEAD_DIM), F32),
                        pltpu.VMEM((ts, GROUP_WIDTH), F32)],
        compiler_params=_cparams(("arbitrary", "arbitrary")),
        name="hgrn2",
    )(p, p, p, p, logf, gain)


def _layer_norm(z, gain, bias):
    mu = jnp.mean(z, axis=-1, keepdims=True)
    zc = z - mu
    var = jnp.mean(zc * zc, axis=-1, keepdims=True)
    return zc * lax.rsqrt(var + NORM_EPS) * gain + bias


def _outproj_kernel(ret_ref, hg_ref, w_ref, x_ref, gain_ref, bias_ref, o_ref):
    mix = jnp.dot(ret_ref[...], w_ref[0:GROUP_WIDTH, :], preferred_element_type=F32)
    mix = mix + jnp.dot(hg_ref[...], w_ref[GROUP_WIDTH:2 * GROUP_WIDTH, :],
                        preferred_element_type=F32)
    z = DEEPNORM_ALPHA * x_ref[...] + mix
    o_ref[...] = _layer_norm(z, gain_ref[...], bias_ref[...])


def _out_proj(ret_o, hg_o, w_bf, x2d, gain, bias):
    n_tok = x2d.shape[0]
    tm = OUTPROJ_TM
    row = lambda w: pl.BlockSpec((tm, w), lambda i: (i, 0))
    full = lambda r, w: pl.BlockSpec((r, w), lambda i: (0, 0))
    return pl.pallas_call(
        _outproj_kernel,
        grid=(n_tok // tm,),
        in_specs=[row(GROUP_WIDTH), row(GROUP_WIDTH), full(2 * GROUP_WIDTH, D_MODEL),
                  row(D_MODEL), full(1, D_MODEL), full(1, D_MODEL)],
        out_specs=row(D_MODEL),
        out_shape=jax.ShapeDtypeStruct((n_tok, D_MODEL), F32),
        compiler_params=_cparams(("arbitrary",)),
        name="out_proj",
    )(ret_o, hg_o, w_bf, x2d, gain, bias)


def _router_kernel(x_ref, w_ref, bias_ref, gate_ref, dest_ref, count_ref, seg_ref,
                   base_ref, eidx_scr, rank_scr):
    tm = ROUTER_TM
    step = pl.program_id(0)
    tok_cols = pl.ds(pl.multiple_of(step * tm, tm), tm)

    @pl.when(step == 0)
    def _():
        base_ref[...] = jnp.zeros_like(base_ref)

    logits = jnp.dot(x_ref[...], w_ref[...], preferred_element_type=F32,
                     precision=lax.Precision.HIGHEST)
    scores = _sigmoid(logits.T)
    sel = scores + bias_ref[...]
    neg = -jnp.inf
    erow = lax.broadcasted_iota(jnp.int32, (N_EXPERTS, tm), 0).astype(F32)
    grow = lax.broadcasted_iota(jnp.int32, (GROUP_SIZE, tm), 0).astype(F32)

    def first_argmax(vals, rows, sentinel):
        m = jnp.max(vals, axis=0, keepdims=True)
        idx = jnp.min(jnp.where(vals == m, rows, sentinel), axis=0, keepdims=True)
        return m, idx

    group_vals, group_score = [], []
    for g in range(N_GROUPS):
        vals = sel[g * GROUP_SIZE:(g + 1) * GROUP_SIZE, :]
        m1, i1 = first_argmax(vals, grow, float(GROUP_SIZE))
        m2 = jnp.max(jnp.where(grow == i1, neg, vals), axis=0, keepdims=True)
        group_vals.append(vals)
        group_score.append(m1 + m2)

    kept = []
    for g in range(N_GROUPS):
        beaten_by = jnp.zeros((1, tm), F32)
        for o in range(N_GROUPS):
            if o == g:
                continue
            wins = (group_score[o] >= group_score[g]) if o < g else (group_score[o] > group_score[g])
            beaten_by = beaten_by + jnp.where(wins, 1.0, 0.0)
        kept.append(jnp.where(beaten_by < TOPK_GROUPS, group_vals[g], neg))
    cur = jnp.concatenate(kept, axis=0)

    idx_rows, gate_rows, onehots = [], [], []
    for kk in range(TOP_K):
        _, idx = first_argmax(cur, erow, float(N_EXPERTS))
        onehot = erow == idx
        gate_rows.append(jnp.sum(jnp.where(onehot, scores, 0.0), axis=0, keepdims=True))
        cur = jnp.where(onehot, neg, cur)
        onehots.append(onehot)
        idx_rows.append(idx)
    gate_t = jnp.concatenate(gate_rows, axis=0)
    gate_ref[...] = gate_t / jnp.sum(gate_t, axis=0, keepdims=True) * ROUTED_SCALE
    eidx_scr[:, tok_cols] = jnp.concatenate(idx_rows, axis=0)

    multihot = jnp.zeros((N_EXPERTS, tm), F32)
    for onehot in onehots:
        multihot = multihot + jnp.where(onehot, 1.0, 0.0)
    trow = lax.broadcasted_iota(jnp.int32, (tm, tm), 0)
    tcol = lax.broadcasted_iota(jnp.int32, (tm, tm), 1)
    earlier = jnp.where(trow < tcol, 1.0, 0.0).astype(BF16)
    before = jnp.dot(multihot.astype(BF16), earlier, preferred_element_type=F32)
    pos = before + base_ref[...]
    rank_rows = [jnp.sum(jnp.where(onehot, pos, 0.0), axis=0, keepdims=True) for onehot in onehots]
    rank_scr[:, tok_cols] = jnp.concatenate(rank_rows, axis=0)
    new_base = base_ref[...] + jnp.sum(multihot, axis=1, keepdims=True)
    base_ref[...] = new_base

    @pl.when(step == pl.num_programs(0) - 1)
    def _():
        counts = new_base
        padded = jnp.floor((counts + (ROW_PAD - 1)) / ROW_PAD) * ROW_PAD
        hi = jnp.floor(padded / 256.0)
        lo = padded - 256.0 * hi
        prow = lax.broadcasted_iota(jnp.int32, (N_EXPERTS, N_EXPERTS), 0)
        pcol = lax.broadcasted_iota(jnp.int32, (N_EXPERTS, N_EXPERTS), 1)
        before_e = jnp.where(pcol < prow, 1.0, 0.0).astype(BF16)
        bcast = lambda v: jnp.broadcast_to(v, (N_EXPERTS, LANES)).astype(BF16)
        seg = (256.0 * jnp.dot(before_e, bcast(hi), preferred_element_type=F32)
               + jnp.dot(before_e, bcast(lo), preferred_element_type=F32))[:, 0:1]
        count_ref[...] = counts.astype(jnp.int32)
        seg_ref[...] = seg.astype(jnp.int32)

        def tile_body(t, carry):
            cols = pl.ds(pl.multiple_of(t * tm, tm), tm)
            eidx = eidx_scr[:, cols]
            starts = [jnp.sum(jnp.where(erow == eidx[kk:kk + 1, :], seg, 0.0), axis=0, keepdims=True)
                      for kk in range(TOP_K)]
            dest = jnp.concatenate(starts, axis=0) + rank_scr[:, cols]
            dest_ref[:, cols] = dest.astype(jnp.int32)
            return carry

        lax.fori_loop(0, pl.num_programs(0), tile_body, 0)


def _router(x1, w_router, bias):
    n_tok = x1.shape[0]
    tm = ROUTER_TM
    row = lambda w: pl.BlockSpec((tm, w), lambda i: (i, 0))
    full = lambda r, w: pl.BlockSpec((r, w), lambda i: (0, 0))
    return pl.pallas_call(
        _router_kernel,
        grid=(n_tok // tm,),
        in_specs=[row(D_MODEL), full(D_MODEL, N_EXPERTS), full(N_EXPERTS, 1)],
        out_specs=[pl.BlockSpec((TOP_K, tm), lambda i: (0, i)), full(TOP_K, n_tok),
                   full(N_EXPERTS, 1), full(N_EXPERTS, 1)],
        out_shape=[jax.ShapeDtypeStruct((TOP_K, n_tok), F32),
                   jax.ShapeDtypeStruct((TOP_K, n_tok), jnp.int32),
                   jax.ShapeDtypeStruct((N_EXPERTS, 1), jnp.int32),
                   jax.ShapeDtypeStruct((N_EXPERTS, 1), jnp.int32)],
        scratch_shapes=[pltpu.VMEM((N_EXPERTS, 1), F32),
                        pltpu.VMEM((TOP_K, n_tok), F32),
                        pltpu.VMEM((TOP_K, n_tok), F32)],
        compiler_params=_cparams(("arbitrary",)),
        name="router",
    )(x1, w_router, bias)


def _sorted_rows(n_tok):
    return n_tok * TOP_K + N_EXPERTS * ROW_PAD + EXPERT_TM


def _padded_count(count):
    return (count + (ROW_PAD - 1)) // ROW_PAD * ROW_PAD


def _dispatch_row_copy(x_ref, xs_hbm, sem, row, dst_row):
    return pltpu.make_async_copy(x_ref.at[pl.ds(row, 1), :], xs_hbm.at[pl.ds(dst_row, 1), :], sem)


def _dispatch_kernel(dest_ref, seg_ref, count_ref, x_ref, xs_hbm, zero_buf, sem, zsem):
    tm = DISPATCH_TM
    i = pl.program_id(0)
    group = SUBLANES

    @pl.when(i == 0)
    def _():
        zero_buf[...] = jnp.zeros_like(zero_buf)

        def pad_copy(e, j):
            row = seg_ref[e] + count_ref[e] + j
            return pltpu.make_async_copy(zero_buf.at[pl.ds(0, 1), :], xs_hbm.at[pl.ds(row, 1), :], zsem)

        def pad_body(e, carry):
            n_pad = _padded_count(count_ref[e]) - count_ref[e]
            for j in range(ROW_PAD - 1):
                @pl.when(j < n_pad)
                def _(j=j):
                    pad_copy(e, j).start()
            for j in range(ROW_PAD - 1):
                @pl.when(j < n_pad)
                def _(j=j):
                    pad_copy(e, j).wait()
            return carry

        lax.fori_loop(0, N_EXPERTS, pad_body, 0)
        last = N_EXPERTS - 1
        n_rows = xs_hbm.shape[0]
        total = seg_ref[last] + _padded_count(count_ref[last])

        def tail_body(j, carry):
            row0 = pl.multiple_of(jnp.minimum(total + j * EXPERT_TM, n_rows - EXPERT_TM), ROW_PAD)
            tail = pltpu.make_async_copy(zero_buf, xs_hbm.at[pl.ds(row0, EXPERT_TM), :], zsem)
            tail.start()
            tail.wait()
            return carry

        lax.fori_loop(0, (n_rows - total + EXPERT_TM - 1) // EXPERT_TM, tail_body, 0)

    n_tok = pl.num_programs(0) * tm

    def issue_body(g, carry):
        for rr in range(group):
            row = g * group + rr
            tok = i * tm + row
            for kk in range(TOP_K):
                _dispatch_row_copy(x_ref, xs_hbm, sem, row, dest_ref[kk * n_tok + tok]).start(
                    priority=kk % N_DMA_QUEUES)
        return carry

    lax.fori_loop(0, tm // group, issue_body, 0)

    def wait_body(g, carry):
        for _ in range(group * TOP_K):
            _dispatch_row_copy(x_ref, xs_hbm, sem, 0, 0).wait()
        return carry

    lax.fori_loop(0, tm // group, wait_body, 0)


def _dispatch(dest_flat, seg, counts, x1):
    n_tok = x1.shape[0]
    tm = DISPATCH_TM
    grid_spec = pltpu.PrefetchScalarGridSpec(
        num_scalar_prefetch=3,
        grid=(n_tok // tm,),
        in_specs=[pl.BlockSpec((tm, D_MODEL), lambda i, d, s, c: (i, 0))],
        out_specs=pl.BlockSpec(memory_space=pl.ANY),
        scratch_shapes=[pltpu.VMEM((EXPERT_TM, D_MODEL), F32),
                        pltpu.SemaphoreType.DMA(()),
                        pltpu.SemaphoreType.DMA(())],
    )
    return pl.pallas_call(
        _dispatch_kernel,
        grid_spec=grid_spec,
        out_shape=jax.ShapeDtypeStruct((_sorted_rows(n_tok), D_MODEL), F32),
        compiler_params=_cparams(("arbitrary",)),
        name="dispatch",
    )(dest_flat, seg, counts, x1)


def _experts_kernel(seg_ref, count_ref, xs_hbm, wg_hbm, wu_hbm, wd_hbm, y_hbm,
                    xbuf, ybuf, wg_buf, wu_buf, wd_buf, in_sem, out_sem, w_sem, pending_ref):
    tm = EXPERT_TM
    e = pl.program_id(0)
    n_experts = pl.num_programs(0)
    n_rows = y_hbm.shape[0]
    count = count_ref[e]
    seg = seg_ref[e]
    n_chunks = (count + tm - 1) // tm
    wslot = e % 2
    extra_slot = 2

    def x_copy(row0, slot):
        return pltpu.make_async_copy(xs_hbm.at[pl.ds(pl.multiple_of(row0, ROW_PAD), tm), :],
                                     xbuf.at[slot], in_sem.at[slot])

    def y_copy(row0, slot):
        return pltpu.make_async_copy(ybuf.at[slot],
                                     y_hbm.at[pl.ds(pl.multiple_of(row0, ROW_PAD), tm), :], out_sem)

    def weight_copies(expert, slot):
        return [pltpu.make_async_copy(src.at[expert], dst.at[slot], w_sem.at[m, slot])
                for m, (src, dst) in enumerate(((wg_hbm, wg_buf), (wu_hbm, wu_buf), (wd_hbm, wd_buf)))]

    def drain_output():
        @pl.when(pending_ref[0] == 1)
        def _():
            y_copy(0, 0).wait()
            pending_ref[0] = 0

    def prefetch_first_chunk(expert):
        @pl.when(count_ref[expert] > 0)
        def _():
            x_copy(seg_ref[expert], expert % 2).start()

    @pl.when(e == 0)
    def _():
        pending_ref[0] = 0
        pending_ref[1] = 0
        for cp in weight_copies(0, 0):
            cp.start(priority=WEIGHT_DMA_PRIORITY)
        prefetch_first_chunk(0)

    @pl.when(e + 1 < n_experts)
    def _():
        nxt = jnp.minimum(e + 1, n_experts - 1)
        for cp in weight_copies(nxt, 1 - wslot):
            cp.start(priority=WEIGHT_DMA_PRIORITY)
        prefetch_first_chunk(nxt)

    for cp in weight_copies(e, wslot):
        cp.wait()

    def chunk_body(c, carry):
        slot = jnp.where(c == 0, wslot, extra_slot)

        @pl.when(c > 0)
        def _():
            x_copy(seg + c * tm, extra_slot).start()

        x_copy(0, slot).wait()
        x = xbuf[slot].astype(BF16)
        g = jnp.dot(x, wg_buf[wslot].astype(BF16), preferred_element_type=F32)
        u = jnp.dot(x, wu_buf[wslot].astype(BF16), preferred_element_type=F32)
        hmid = (_silu(g) * u).astype(BF16)
        yslot = pending_ref[1]
        ybuf[yslot] = jnp.dot(hmid, wd_buf[wslot].astype(BF16), preferred_element_type=F32)
        drain_output()
        y_copy(seg + c * tm, yslot).start()
        pending_ref[0] = 1
        pending_ref[1] = 1 - yslot
        return carry

    lax.fori_loop(0, n_chunks, chunk_body, 0)

    @pl.when(e == n_experts - 1)
    def _():
        drain_output()
        total = seg + _padded_count(count)
        ybuf[0] = jnp.zeros((tm, D_MODEL), F32)
        n_fill = (n_rows - total + tm - 1) // tm

        def fill_body(j, carry):
            fill = y_copy(jnp.minimum(total + j * tm, n_rows - tm), 0)
            fill.start()
            fill.wait()
            return carry

        lax.fori_loop(0, n_fill, fill_body, 0)


def _experts(seg, counts, xs, w_gate, w_up, w_down):
    n_rows = xs.shape[0]
    tm = EXPERT_TM
    hbm = pl.BlockSpec(memory_space=pl.ANY)
    grid_spec = pltpu.PrefetchScalarGridSpec(
        num_scalar_prefetch=2,
        grid=(N_EXPERTS,),
        in_specs=[hbm, hbm, hbm, hbm],
        out_specs=hbm,
        scratch_shapes=[pltpu.VMEM((3, tm, D_MODEL), F32),
                        pltpu.VMEM((2, tm, D_MODEL), F32),
                        pltpu.VMEM((2, D_MODEL, EXPERT_FF), F32),
                        pltpu.VMEM((2, D_MODEL, EXPERT_FF), F32),
                        pltpu.VMEM((2, EXPERT_FF, D_MODEL), F32),
                        pltpu.SemaphoreType.DMA((3,)),
                        pltpu.SemaphoreType.DMA(()),
                        pltpu.SemaphoreType.DMA((3, 2)),
                        pltpu.SMEM((2,), jnp.int32)],
    )
    return pl.pallas_call(
        _experts_kernel,
        grid_spec=grid_spec,
        out_shape=jax.ShapeDtypeStruct((n_rows, D_MODEL), F32),
        compiler_params=_cparams(("arbitrary",)),
        name="experts",
    )(seg, counts, xs, w_gate, w_up, w_down)


def _y_copy(y_hbm, ybuf, sem, src_row, slot, kk, r):
    return pltpu.make_async_copy(y_hbm.at[pl.ds(src_row, 1), :],
                                 ybuf.at[slot, kk, pl.ds(r, 1), :], sem.at[slot])


def _combine_kernel(dest_ref, y_hbm, x_ref, gate_ref, wsg_ref, wsu_ref, wsd_ref,
                    gain_ref, bias_ref, o_ref, ybuf, sem):
    tm = COMBINE_TM
    i = pl.program_id(0)
    n_tiles = pl.num_programs(0)
    slot = i % 2

    group = SUBLANES
    n_tok = n_tiles * tm

    def issue(tile, dst_slot):
        def body(g, carry):
            for rr in range(group):
                r = g * group + rr
                tok = tile * tm + r
                for kk in range(TOP_K):
                    _y_copy(y_hbm, ybuf, sem, dest_ref[kk * n_tok + tok], dst_slot, kk, r).start(
                        priority=kk % N_DMA_QUEUES)
            return carry
        lax.fori_loop(0, tm // group, body, 0)

    def wait_all(dst_slot):
        def body(g, carry):
            for _ in range(group * TOP_K):
                _y_copy(y_hbm, ybuf, sem, 0, dst_slot, 0, 0).wait()
            return carry
        lax.fori_loop(0, tm // group, body, 0)

    @pl.when(i == 0)
    def _():
        issue(0, 0)

    @pl.when(i + 1 < n_tiles)
    def _():
        issue(i + 1, 1 - slot)

    x = x_ref[...]
    xb = x.astype(BF16)
    g = jnp.dot(xb, wsg_ref[...], preferred_element_type=F32)
    u = jnp.dot(xb, wsu_ref[...], preferred_element_type=F32)
    shared = jnp.dot((_silu(g) * u).astype(BF16), wsd_ref[...], preferred_element_type=F32)

    wait_all(slot)
    gate_t = gate_ref[...]
    diag = (lax.broadcasted_iota(jnp.int32, (tm, tm), 0)
            == lax.broadcasted_iota(jnp.int32, (tm, tm), 1))
    routed = jnp.zeros((tm, D_MODEL), F32)
    for kk in range(TOP_K):
        gate_col = jnp.sum(jnp.where(diag, gate_t[kk:kk + 1, :], 0.0), axis=1, keepdims=True)
        routed = routed + gate_col * ybuf[slot, kk]
    z = DEEPNORM_ALPHA * x + (routed + shared)
    o_ref[...] = _layer_norm(z, gain_ref[...], bias_ref[...])


def _combine(dest_flat, y_rows, x1, gate, wsg, wsu, wsd, gain, bias):
    n_tok = x1.shape[0]
    tm = COMBINE_TM
    row = lambda w: pl.BlockSpec((tm, w), lambda i, d: (i, 0))
    full = lambda r, w: pl.BlockSpec((r, w), lambda i, d: (0, 0))
    grid_spec = pltpu.PrefetchScalarGridSpec(
        num_scalar_prefetch=1,
        grid=(n_tok // tm,),
        in_specs=[pl.BlockSpec(memory_space=pl.ANY), row(D_MODEL),
                  pl.BlockSpec((TOP_K, tm), lambda i, d: (0, i)),
                  full(D_MODEL, SHARED_FF), full(D_MODEL, SHARED_FF), full(SHARED_FF, D_MODEL),
                  full(1, D_MODEL), full(1, D_MODEL)],
        out_specs=row(D_MODEL),
        scratch_shapes=[pltpu.VMEM((2, TOP_K, tm, D_MODEL), F32),
                        pltpu.SemaphoreType.DMA((2,))],
    )
    return pl.pallas_call(
        _combine_kernel,
        grid_spec=grid_spec,
        out_shape=jax.ShapeDtypeStruct((n_tok, D_MODEL), F32),
        compiler_params=_cparams(("arbitrary",)),
        name="combine",
    )(dest_flat, y_rows, x1, gate, wsg, wsu, wsd, gain, bias)


def _rope_tables(seq):
    half = HEAD_DIM // 2
    pos = jnp.arange(seq, dtype=F32)
    inv_freq = ROPE_BASE ** (-jnp.arange(half, dtype=F32) / half)
    ang = pos[:, None] * inv_freq[None, :]
    cos, sin = jnp.cos(ang), jnp.sin(ang)
    return jnp.concatenate([cos, cos], axis=-1), jnp.concatenate([-sin, sin], axis=-1)


def kernel(x, w_in, ret_gn_gain, hgrn_lb_logits, hgrn_norm_gain, w_out, ln1_gain, ln1_bias,
           w_router, router_bias, w_gate, w_up, w_down, ws_gate, ws_up, ws_down,
           ln2_gain, ln2_bias):
    batch, seq, d = x.shape
    n_tok = batch * seq
    cos_t, sin_t = _rope_tables(seq)
    lower_bounds = jnp.cumsum(jax.nn.softmax(hgrn_lb_logits.astype(F32), axis=0), axis=0)
    x2d = x.reshape(n_tok, d)
    for l in range(DEPTH):
        p, logf = _in_proj(x2d.astype(BF16), w_in[l].astype(BF16), cos_t, sin_t,
                           lower_bounds[l].reshape(1, GROUP_WIDTH), seq)
        ret_o = _retention(p, ret_gn_gain[l].reshape(1, GROUP_WIDTH), batch, seq)
        hg_o = _hgrn2(p, logf, hgrn_norm_gain[l].reshape(1, GROUP_WIDTH), batch, seq)
        x1 = _out_proj(ret_o, hg_o, w_out[l].astype(BF16), x2d,
                       ln1_gain[l].reshape(1, d), ln1_bias[l].reshape(1, d))
        gate, dest, counts, seg = _router(x1, w_router[l], router_bias[l].reshape(N_EXPERTS, 1))
        dest_flat = dest.reshape(TOP_K * n_tok)
        counts, seg = counts.reshape(N_EXPERTS), seg.reshape(N_EXPERTS)
        x_sorted = _dispatch(dest_flat, seg, counts, x1)
        y_rows = _experts(seg, counts, x_sorted, w_gate[l], w_up[l], w_down[l])
        x2d = _combine(dest_flat, y_rows, x1, gate, ws_gate[l].astype(BF16),
                       ws_up[l].astype(BF16), ws_down[l].astype(BF16),
                       ln2_gain[l].reshape(1, d), ln2_bias[l].reshape(1, d))
    return x2d.reshape(batch, seq, d)
```

```python
import functools

import numpy as np
import jax
import jax.numpy as jnp
from jax import lax
from jax.experimental import pallas as pl
from jax.experimental.pallas import tpu as pltpu

D_MODEL = 2048
N_HEADS = 8
HEAD_DIM = 128
GROUP_WIDTH = N_HEADS * HEAD_DIM
N_SECTIONS = 8
ROPE_BASE = 10000.0
N_EXPERTS = 256
TOP_K = 8
N_GROUPS = 8
GROUP_SIZE = N_EXPERTS // N_GROUPS
TOPK_GROUPS = 4
EXPERT_FF = 512
SHARED_FF = 512
ROUTED_SCALE = 2.5
NORM_EPS = 1e-5
DEPTH = 1
DEEPNORM_ALPHA = (2.0 * DEPTH) ** 0.25

V7X_VMEM_LIMIT_BYTES = 56 * 1024 * 1024
LANES = 128
SUBLANES = 8
ROW_TILES = D_MODEL // LANES

INPROJ_TM = 1024
INPROJ_TN = 256
MIX_TS = 512
RET_CHUNK = 128
HG_CHUNK = 64
HG_DIRECT = 8
OUTPROJ_TM = 256
ROUTER_TM = 256
DISPATCH_TM = 256
EXPERT_TM = 320
WEIGHT_DMA_PRIORITY = 1
N_DMA_QUEUES = 2
ROW_PAD = SUBLANES
COMBINE_TM = 128

F32 = jnp.float32
BF16 = jnp.bfloat16


def _sigmoid(v):
    return 1.0 / (1.0 + jnp.exp(-v))


def _silu(v):
    return v * _sigmoid(v)


def _cparams(semantics):
    return pltpu.CompilerParams(dimension_semantics=semantics,
                                vmem_limit_bytes=V7X_VMEM_LIMIT_BYTES)


def _inproj_kernel(x_ref, w_ref, cos_ref, sin_ref, lb_ref, p_ref, logf_ref):
    j = pl.program_id(1)
    x = x_ref[...]

    def column_blocks():
        for c in range(GROUP_WIDTH // INPROJ_TN):
            cols = slice(c * INPROJ_TN, (c + 1) * INPROJ_TN)
            yield cols, jnp.dot(x, w_ref[:, cols], preferred_element_type=F32)

    @pl.when(j < 2)
    def _rotary():
        scale = jnp.where(j == 1, HEAD_DIM ** -0.5, 1.0).astype(F32)
        cos = cos_ref[...]
        sin = sin_ref[...]
        for cols, acc in column_blocks():
            for h in range(INPROJ_TN // HEAD_DIM):
                t = acc[:, h * HEAD_DIM:(h + 1) * HEAD_DIM]
                r = pltpu.roll(t, HEAD_DIM // 2, axis=1)
                lo = cols.start + h * HEAD_DIM
                p_ref[:, lo:lo + HEAD_DIM] = ((t * cos + r * sin) * scale).astype(BF16)

    @pl.when(j == 5)
    def _forget():
        for cols, acc in column_blocks():
            lb = lb_ref[:, cols]
            f = lb + (1.0 - lb) * _sigmoid(acc)
            logf_ref[:, cols] = jnp.log(f)
            p_ref[:, cols] = (1.0 - f).astype(BF16)

    @pl.when(jnp.logical_and(j >= 2, j != 5))
    def _plain():
        for cols, acc in column_blocks():
            p_ref[:, cols] = acc.astype(BF16)


def _in_proj(x_bf, w_bf, cos_t, sin_t, lb, seq):
    n_tok = x_bf.shape[0]
    tm = INPROJ_TM
    pos_blocks = seq // tm
    return pl.pallas_call(
        _inproj_kernel,
        grid=(n_tok // tm, N_SECTIONS),
        in_specs=[
            pl.BlockSpec((tm, D_MODEL), lambda i, j: (i, 0)),
            pl.BlockSpec((D_MODEL, GROUP_WIDTH), lambda i, j: (0, j)),
            pl.BlockSpec((tm, HEAD_DIM), lambda i, j: (i % pos_blocks, 0)),
            pl.BlockSpec((tm, HEAD_DIM), lambda i, j: (i % pos_blocks, 0)),
            pl.BlockSpec((1, GROUP_WIDTH), lambda i, j: (0, 0)),
        ],
        out_specs=[
            pl.BlockSpec((tm, GROUP_WIDTH), lambda i, j: (i, j)),
            pl.BlockSpec((tm, GROUP_WIDTH), lambda i, j: (i, 0)),
        ],
        out_shape=[
            jax.ShapeDtypeStruct((n_tok, N_SECTIONS * GROUP_WIDTH), BF16),
            jax.ShapeDtypeStruct((n_tok, GROUP_WIDTH), F32),
        ],
        compiler_params=_cparams(("arbitrary", "arbitrary")),
        name="in_proj",
    )(x_bf, w_bf, cos_t, sin_t, lb)


def _ret_log_decay(h):
    return float(np.log(np.float32(1.0) - np.float32(2.0) ** np.float32(-5.0 - h)))


def _retention_kernel(q_ref, k_ref, v_ref, g_ref, gain_ref, o_ref,
                      decay_ref, qscale_ref, kscale_ref, *state_refs):
    c = RET_CHUNK

    @pl.when(pl.program_id(1) == 0)
    def _():
        for state_ref in state_refs:
            state_ref[...] = jnp.zeros_like(state_ref)

    row = lax.broadcasted_iota(jnp.int32, (c, c), 0)
    col = lax.broadcasted_iota(jnp.int32, (c, c), 1)
    causal = row >= col
    delta = jnp.where(causal, row - col, 0).astype(F32)
    rvec = lax.broadcasted_iota(jnp.int32, (c, HEAD_DIM), 0).astype(F32)
    for h in range(N_HEADS):
        ld = _ret_log_decay(h)
        decay_ref[h] = jnp.where(causal, jnp.exp(delta * ld), 0.0)
        qscale_ref[h] = jnp.exp((rvec + 1.0) * ld)
        kscale_ref[h] = jnp.exp((c - 1.0 - rvec) * ld)

    def chunk_body(ci, carry):
        rows = pl.ds(pl.multiple_of(ci * c, c), c)
        for h in range(N_HEADS):
            hs = slice(h * HEAD_DIM, (h + 1) * HEAD_DIM)
            chunk_decay = float(np.exp(np.float32(_ret_log_decay(h)) * np.float32(c)))
            q = q_ref[rows, hs]
            k = k_ref[rows, hs]
            v = v_ref[rows, hs]
            state = state_refs[h][...]
            scores = lax.dot_general(q, k, (((1,), (1,)), ((), ())),
                                     preferred_element_type=F32) * decay_ref[h]
            o = jnp.dot(scores.astype(BF16), v, preferred_element_type=F32)
            o = o + qscale_ref[h] * jnp.dot(q, state.astype(BF16), preferred_element_type=F32)
            k_dec = (k.astype(F32) * kscale_ref[h]).astype(BF16)
            kv = lax.dot_general(k_dec, v, (((0,), (0,)), ((), ())),
                                 preferred_element_type=F32)
            state_refs[h][...] = chunk_decay * state + kv
            o = o - jnp.mean(o, axis=-1, keepdims=True)
            o = o * lax.rsqrt(jnp.mean(o * o, axis=-1, keepdims=True) + NORM_EPS)
            gate = g_ref[rows, hs].astype(F32)
            o_ref[rows, hs] = (o * gain_ref[:, hs] * _silu(gate)).astype(BF16)
        return carry

    lax.fori_loop(0, MIX_TS // c, chunk_body, 0)


def _retention(p, gain, batch, seq):
    n_tok = p.shape[0]
    ts = MIX_TS
    tpb = seq // ts
    c = RET_CHUNK
    sec = lambda s: pl.BlockSpec((ts, GROUP_WIDTH), lambda b, t, s=s: (b * tpb + t, s))
    return pl.pallas_call(
        _retention_kernel,
        grid=(batch, tpb),
        in_specs=[sec(0), sec(1), sec(2), sec(3),
                  pl.BlockSpec((1, GROUP_WIDTH), lambda b, t: (0, 0))],
        out_specs=pl.BlockSpec((ts, GROUP_WIDTH), lambda b, t: (b * tpb + t, 0)),
        out_shape=jax.ShapeDtypeStruct((n_tok, GROUP_WIDTH), BF16),
        scratch_shapes=[pltpu.VMEM((N_HEADS, c, c), F32),
                        pltpu.VMEM((N_HEADS, c, HEAD_DIM), F32),
                        pltpu.VMEM((N_HEADS, c, HEAD_DIM), F32)]
                       + [pltpu.VMEM((HEAD_DIM, HEAD_DIM), F32) for _ in range(N_HEADS)],
        compiler_params=_cparams(("arbitrary", "arbitrary")),
        name="retention",
    )(p, p, p, p, gain)


def _hgrn2_kernel(q_ref, k_ref, v_ref, g_ref, logf_ref, gain_ref, o_ref, state_ref, cum_ref):
    c = HG_CHUNK
    ts = MIX_TS

    @pl.when(pl.program_id(1) == 0)
    def _():
        state_ref[...] = jnp.zeros_like(state_ref)

    trow = lax.broadcasted_iota(jnp.int32, (ts, 1), 0) % c
    cum = logf_ref[...]
    sh = 1
    while sh < c:
        cum = cum + jnp.where(trow >= sh, pltpu.roll(cum, sh, axis=0), 0.0)
        sh *= 2
    cum_ref[...] = cum

    crow = lax.broadcasted_iota(jnp.int32, (c, 1), 0)
    srow = lax.broadcasted_iota(jnp.int32, (c, c), 0)
    scol = lax.broadcasted_iota(jnp.int32, (c, c), 1)
    drow = lax.broadcasted_iota(jnp.int32, (HG_DIRECT, 1), 0)
    ones_bf = jnp.ones((HEAD_DIM, HEAD_DIM), BF16)
    n_direct = c // HG_DIRECT

    def chunk_body(ci, carry):
        rows = pl.ds(pl.multiple_of(ci * c, c), c)
        for h in range(N_HEADS):
            hs = slice(h * HEAD_DIM, (h + 1) * HEAD_DIM)
            q = q_ref[rows, hs].astype(F32)
            k = k_ref[rows, hs].astype(F32)
            v_bf = v_ref[rows, hs]
            v = v_bf.astype(F32)
            b = cum_ref[rows, hs]
            b_last = b[c - 1:c, :]
            state_t = state_ref[h]

            q_in = (q * jnp.exp(b)).astype(BF16)
            o = lax.dot_general(q_in, state_t.astype(BF16), (((1,), (1,)), ((), ())),
                                preferred_element_type=F32)

            scores = jnp.zeros((c, c), F32)
            level = 2 * HG_DIRECT
            while level <= c:
                half = level // 2
                anchors = []
                for blk in range(c // level):
                    a = blk * level + half - 1
                    anchors.append(jnp.broadcast_to(b[a:a + 1, :], (level, HEAD_DIM)))
                anchor = anchors[0] if len(anchors) == 1 else jnp.concatenate(anchors, axis=0)
                upper = (crow % level) >= half
                q_l = jnp.where(upper, q * jnp.exp(jnp.minimum(b - anchor, 0.0)), 0.0)
                k_l = jnp.where(upper, 0.0, k * jnp.exp(jnp.minimum(anchor - b, 0.0)))
                s_l = lax.dot_general(q_l.astype(BF16), k_l.astype(BF16),
                                      (((1,), (1,)), ((), ())), preferred_element_type=F32)
                same_block = (srow // level) == (scol // level)
                scores = scores + jnp.where(same_block, s_l, 0.0)
                level *= 2
            o = o + jnp.dot(scores.astype(BF16), v_bf, preferred_element_type=F32)

            prods = []
            for blk in range(n_direct):
                r0 = blk * HG_DIRECT
                qb = q[r0:r0 + HG_DIRECT, :]
                bb = b[r0:r0 + HG_DIRECT, :]
                for s in range(HG_DIRECT):
                    dec = jnp.exp(jnp.where(drow >= s, bb - b[r0 + s:r0 + s + 1, :], -jnp.inf))
                    prods.append(qb * dec * k[r0 + s:r0 + s + 1, :])
            prod = jnp.concatenate(prods, axis=0).astype(BF16)
            rowsum = jnp.dot(prod, ones_bf, preferred_element_type=F32)
            direct = []
            for blk in range(n_direct):
                r0 = blk * HG_DIRECT
                acc = jnp.zeros((HG_DIRECT, HEAD_DIM), F32)
                for s in range(HG_DIRECT):
                    p0 = (blk * HG_DIRECT + s) * HG_DIRECT
                    acc = acc + rowsum[p0:p0 + HG_DIRECT, :] * v[r0 + s:r0 + s + 1, :]
                direct.append(acc)
            o = o + jnp.concatenate(direct, axis=0)

            k_dec = (k * jnp.exp(b_last - b)).astype(BF16)
            vk = lax.dot_general(v_bf, k_dec, (((0,), (0,)), ((), ())),
                                 preferred_element_type=F32)
            state_ref[h] = state_t * jnp.exp(b_last) + vk

            o = o * lax.rsqrt(jnp.mean(o * o, axis=-1, keepdims=True) + NORM_EPS)
            gate = g_ref[rows, hs].astype(F32)
            o_ref[rows, hs] = (o * gain_ref[:, hs] * _silu(gate)).astype(BF16)
        return carry

    lax.fori_loop(0, ts // c, chunk_body, 0)


def _hgrn2(p, logf, gain, batch, seq):
    n_tok = p.shape[0]
    ts = MIX_TS
    tpb = seq // ts
    sec = lambda s: pl.BlockSpec((ts, GROUP_WIDTH), lambda b, t, s=s: (b * tpb + t, s))
    return pl.pallas_call(
        _hgrn2_kernel,
        grid=(batch, tpb),
        in_specs=[sec(4), sec(5), sec(6), sec(7),
                  pl.BlockSpec((ts, GROUP_WIDTH), lambda b, t: (b * tpb + t, 0)),
                  pl.BlockSpec((1, GROUP_WIDTH), lambda b, t: (0, 0))],
        out_specs=pl.BlockSpec((ts, GROUP_WIDTH), lambda b, t: (b * tpb + t, 0)),
        out_shape=jax.ShapeDtypeStruct((n_tok, GROUP_WIDTH), BF16),
        scratch_shapes=[pltpu.VMEM((N_HEADS, HEAD_DIM, HEAD_DIM), F32),
                        pltpu.VMEM((ts, GROUP_WIDTH), F32)],
        compiler_params=_cparams(("arbitrary", "arbitrary")),
        name="hgrn2",
    )(p, p, p, p, logf, gain)


def _layer_norm(z, gain, bias):
    mu = jnp.mean(z, axis=-1, keepdims=True)
    zc = z - mu
    var = jnp.mean(zc * zc, axis=-1, keepdims=True)
    return zc * lax.rsqrt(var + NORM_EPS) * gain + bias


def _outproj_kernel(ret_ref, hg_ref, w_ref, x_ref, gain_ref, bias_ref, o_ref):
    mix = jnp.dot(ret_ref[...], w_ref[0:GROUP_WIDTH, :], preferred_element_type=F32)
    mix = mix + jnp.dot(hg_ref[...], w_ref[GROUP_WIDTH:2 * GROUP_WIDTH, :],
                        preferred_element_type=F32)
    z = DEEPNORM_ALPHA * x_ref[...] + mix
    o_ref[...] = _layer_norm(z, gain_ref[...], bias_ref[...])


def _out_proj(ret_o, hg_o, w_bf, x2d, gain, bias):
    n_tok = x2d.shape[0]
    tm = OUTPROJ_TM
    row = lambda w: pl.BlockSpec((tm, w), lambda i: (i, 0))
    full = lambda r, w: pl.BlockSpec((r, w), lambda i: (0, 0))
    return pl.pallas_call(
        _outproj_kernel,
        grid=(n_tok // tm,),
        in_specs=[row(GROUP_WIDTH), row(GROUP_WIDTH), full(2 * GROUP_WIDTH, D_MODEL),
                  row(D_MODEL), full(1, D_MODEL), full(1, D_MODEL)],
        out_specs=row(D_MODEL),
        out_shape=jax.ShapeDtypeStruct((n_tok, D_MODEL), F32),
        compiler_params=_cparams(("arbitrary",)),
        name="out_proj",
    )(ret_o, hg_o, w_bf, x2d, gain, bias)


def _router_kernel(x_ref, w_ref, bias_ref, gate_ref, dest_ref, count_ref, seg_ref,
                   base_ref, eidx_scr, rank_scr):
    tm = ROUTER_TM
    step = pl.program_id(0)
    tok_cols = pl.ds(pl.multiple_of(step * tm, tm), tm)

    @pl.when(step == 0)
    def _():
        base_ref[...] = jnp.zeros_like(base_ref)

    logits = jnp.dot(x_ref[...], w_ref[...], preferred_element_type=F32,
                     precision=lax.Precision.HIGHEST)
    scores = _sigmoid(logits.T)
    sel = scores + bias_ref[...]
    neg = -jnp.inf
    erow = lax.broadcasted_iota(jnp.int32, (N_EXPERTS, tm), 0).astype(F32)
    grow = lax.broadcasted_iota(jnp.int32, (GROUP_SIZE, tm), 0).astype(F32)

    def first_argmax(vals, rows, sentinel):
        m = jnp.max(vals, axis=0, keepdims=True)
        idx = jnp.min(jnp.where(vals == m, rows, sentinel), axis=0, keepdims=True)
        return m, idx

    group_vals, group_score = [], []
    for g in range(N_GROUPS):
        vals = sel[g * GROUP_SIZE:(g + 1) * GROUP_SIZE, :]
        m1, i1 = first_argmax(vals, grow, float(GROUP_SIZE))
        m2 = jnp.max(jnp.where(grow == i1, neg, vals), axis=0, keepdims=True)
        group_vals.append(vals)
        group_score.append(m1 + m2)

    kept = []
    for g in range(N_GROUPS):
        beaten_by = jnp.zeros((1, tm), F32)
        for o in range(N_GROUPS):
            if o == g:
                continue
            wins = (group_score[o] >= group_score[g]) if o < g else (group_score[o] > group_score[g])
            beaten_by = beaten_by + jnp.where(wins, 1.0, 0.0)
        kept.append(jnp.where(beaten_by < TOPK_GROUPS, group_vals[g], neg))
    cur = jnp.concatenate(kept, axis=0)

    idx_rows, gate_rows, onehots = [], [], []
    for kk in range(TOP_K):
        _, idx = first_argmax(cur, erow, float(N_EXPERTS))
        onehot = erow == idx
        gate_rows.append(jnp.sum(jnp.where(onehot, scores, 0.0), axis=0, keepdims=True))
        cur = jnp.where(onehot, neg, cur)
        onehots.append(onehot)
        idx_rows.append(idx)
    gate_t = jnp.concatenate(gate_rows, axis=0)
    gate_ref[...] = gate_t / jnp.sum(gate_t, axis=0, keepdims=True) * ROUTED_SCALE
    eidx_scr[:, tok_cols] = jnp.concatenate(idx_rows, axis=0)

    multihot = jnp.zeros((N_EXPERTS, tm), F32)
    for onehot in onehots:
        multihot = multihot + jnp.where(onehot, 1.0, 0.0)
    trow = lax.broadcasted_iota(jnp.int32, (tm, tm), 0)
    tcol = lax.broadcasted_iota(jnp.int32, (tm, tm), 1)
    earlier = jnp.where(trow < tcol, 1.0, 0.0).astype(BF16)
    before = jnp.dot(multihot.astype(BF16), earlier, preferred_element_type=F32)
    pos = before + base_ref[...]
    rank_rows = [jnp.sum(jnp.where(onehot, pos, 0.0), axis=0, keepdims=True) for onehot in onehots]
    rank_scr[:, tok_cols] = jnp.concatenate(rank_rows, axis=0)
    new_base = base_ref[...] + jnp.sum(multihot, axis=1, keepdims=True)
    base_ref[...] = new_base

    @pl.when(step == pl.num_programs(0) - 1)
    def _():
        counts = new_base
        padded = jnp.floor((counts + (ROW_PAD - 1)) / ROW_PAD) * ROW_PAD
        hi = jnp.floor(padded / 256.0)
        lo = padded - 256.0 * hi
        prow = lax.broadcasted_iota(jnp.int32, (N_EXPERTS, N_EXPERTS), 0)
        pcol = lax.broadcasted_iota(jnp.int32, (N_EXPERTS, N_EXPERTS), 1)
        before_e = jnp.where(pcol < prow, 1.0, 0.0).astype(BF16)
        bcast = lambda v: jnp.broadcast_to(v, (N_EXPERTS, LANES)).astype(BF16)
        seg = (256.0 * jnp.dot(before_e, bcast(hi), preferred_element_type=F32)
               + jnp.dot(before_e, bcast(lo), preferred_element_type=F32))[:, 0:1]
        count_ref[...] = counts.astype(jnp.int32)
        seg_ref[...] = seg.astype(jnp.int32)

        def tile_body(t, carry):
            cols = pl.ds(pl.multiple_of(t * tm, tm), tm)
            eidx = eidx_scr[:, cols]
            starts = [jnp.sum(jnp.where(erow == eidx[kk:kk + 1, :], seg, 0.0), axis=0, keepdims=True)
                      for kk in range(TOP_K)]
            dest = jnp.concatenate(starts, axis=0) + rank_scr[:, cols]
            dest_ref[:, cols] = dest.astype(jnp.int32)
            return carry

        lax.fori_loop(0, pl.num_programs(0), tile_body, 0)


def _router(x1, w_router, bias):
    n_tok = x1.shape[0]
    tm = ROUTER_TM
    row = lambda w: pl.BlockSpec((tm, w), lambda i: (i, 0))
    full = lambda r, w: pl.BlockSpec((r, w), lambda i: (0, 0))
    return pl.pallas_call(
        _router_kernel,
        grid=(n_tok // tm,),
        in_specs=[row(D_MODEL), full(D_MODEL, N_EXPERTS), full(N_EXPERTS, 1)],
        out_specs=[pl.BlockSpec((TOP_K, tm), lambda i: (0, i)), full(TOP_K, n_tok),
                   full(N_EXPERTS, 1), full(N_EXPERTS, 1)],
        out_shape=[jax.ShapeDtypeStruct((TOP_K, n_tok), F32),
                   jax.ShapeDtypeStruct((TOP_K, n_tok), jnp.int32),
                   jax.ShapeDtypeStruct((N_EXPERTS, 1), jnp.int32),
                   jax.ShapeDtypeStruct((N_EXPERTS, 1), jnp.int32)],
        scratch_shapes=[pltpu.VMEM((N_EXPERTS, 1), F32),
                        pltpu.VMEM((TOP_K, n_tok), F32),
                        pltpu.VMEM((TOP_K, n_tok), F32)],
        compiler_params=_cparams(("arbitrary",)),
        name="router",
    )(x1, w_router, bias)


def _sorted_rows(n_tok):
    return n_tok * TOP_K + N_EXPERTS * ROW_PAD + EXPERT_TM


def _padded_count(count):
    return (count + (ROW_PAD - 1)) // ROW_PAD * ROW_PAD


def _dispatch_row_copy(x_tiles, xs_hbm, sem, row, dst_row):
    return pltpu.make_async_copy(x_tiles.at[row], xs_hbm.at[dst_row], sem)


def _dispatch_kernel(dest_ref, seg_ref, count_ref, x_ref, xs_hbm, x_tiles, zero_buf, sem, zsem):
    tm = DISPATCH_TM
    i = pl.program_id(0)
    group = SUBLANES

    @pl.when(i == 0)
    def _():
        zero_buf[...] = jnp.zeros_like(zero_buf)

        def pad_copy(e, j):
            row = seg_ref[e] + count_ref[e] + j
            return pltpu.make_async_copy(zero_buf.at[0], xs_hbm.at[row], zsem)

        def pad_body(e, carry):
            n_pad = _padded_count(count_ref[e]) - count_ref[e]
            for j in range(ROW_PAD - 1):
                @pl.when(j < n_pad)
                def _(j=j):
                    pad_copy(e, j).start()
            for j in range(ROW_PAD - 1):
                @pl.when(j < n_pad)
                def _(j=j):
                    pad_copy(e, j).wait()
            return carry

        lax.fori_loop(0, N_EXPERTS, pad_body, 0)
        last = N_EXPERTS - 1
        n_rows = xs_hbm.shape[0]
        total = seg_ref[last] + _padded_count(count_ref[last])

        def tail_body(j, carry):
            row0 = jnp.minimum(total + j * EXPERT_TM, n_rows - EXPERT_TM)
            tail = pltpu.make_async_copy(zero_buf, xs_hbm.at[pl.ds(row0, EXPERT_TM)], zsem)
            tail.start()
            tail.wait()
            return carry

        lax.fori_loop(0, (n_rows - total + EXPERT_TM - 1) // EXPERT_TM, tail_body, 0)

    n_tok = pl.num_programs(0) * tm
    for s in range(ROW_TILES):
        x_tiles[:, s, :] = x_ref[:, s * LANES:(s + 1) * LANES]

    def issue_body(g, carry):
        for rr in range(group):
            row = g * group + rr
            tok = i * tm + row
            for kk in range(TOP_K):
                _dispatch_row_copy(x_tiles, xs_hbm, sem, row, dest_ref[kk * n_tok + tok]).start(
                    priority=kk % N_DMA_QUEUES)
        return carry

    lax.fori_loop(0, tm // group, issue_body, 0)

    def wait_body(g, carry):
        for _ in range(group * TOP_K):
            _dispatch_row_copy(x_tiles, xs_hbm, sem, 0, 0).wait()
        return carry

    lax.fori_loop(0, tm // group, wait_body, 0)


def _dispatch(dest_flat, seg, counts, x1):
    n_tok = x1.shape[0]
    tm = DISPATCH_TM
    grid_spec = pltpu.PrefetchScalarGridSpec(
        num_scalar_prefetch=3,
        grid=(n_tok // tm,),
        in_specs=[pl.BlockSpec((tm, D_MODEL), lambda i, d, s, c: (i, 0))],
        out_specs=pl.BlockSpec(memory_space=pl.ANY),
        scratch_shapes=[pltpu.VMEM((tm, ROW_TILES, LANES), F32),
                        pltpu.VMEM((EXPERT_TM, ROW_TILES, LANES), F32),
                        pltpu.SemaphoreType.DMA(()),
                        pltpu.SemaphoreType.DMA(())],
    )
    return pl.pallas_call(
        _dispatch_kernel,
        grid_spec=grid_spec,
        out_shape=jax.ShapeDtypeStruct((_sorted_rows(n_tok), ROW_TILES, LANES), F32),
        compiler_params=_cparams(("arbitrary",)),
        name="dispatch",
    )(dest_flat, seg, counts, x1)


def _experts_kernel(seg_ref, count_ref, xs_hbm, wg_hbm, wu_hbm, wd_hbm, y_hbm,
                    xbuf, ybuf, wg_buf, wu_buf, wd_buf, in_sem, out_sem, w_sem, pending_ref):
    tm = EXPERT_TM
    e = pl.program_id(0)
    n_experts = pl.num_programs(0)
    n_rows = y_hbm.shape[0]
    count = count_ref[e]
    seg = seg_ref[e]
    n_chunks = (count + tm - 1) // tm
    wslot = e % 2
    extra_slot = 2

    def x_copy(row0, slot):
        return pltpu.make_async_copy(xs_hbm.at[pl.ds(row0, tm)], xbuf.at[slot], in_sem.at[slot])

    def y_copy(row0, slot):
        return pltpu.make_async_copy(ybuf.at[slot],
                                     y_hbm.at[pl.ds(pl.multiple_of(row0, ROW_PAD), tm), :], out_sem)

    def weight_copies(expert, slot):
        return [pltpu.make_async_copy(src.at[expert], dst.at[slot], w_sem.at[m, slot])
                for m, (src, dst) in enumerate(((wg_hbm, wg_buf), (wu_hbm, wu_buf), (wd_hbm, wd_buf)))]

    def drain_output():
        @pl.when(pending_ref[0] == 1)
        def _():
            y_copy(0, 0).wait()
            pending_ref[0] = 0

    def prefetch_first_chunk(expert):
        @pl.when(count_ref[expert] > 0)
        def _():
            x_copy(seg_ref[expert], expert % 2).start()

    @pl.when(e == 0)
    def _():
        pending_ref[0] = 0
        pending_ref[1] = 0
        for cp in weight_copies(0, 0):
            cp.start(priority=WEIGHT_DMA_PRIORITY)
        prefetch_first_chunk(0)

    @pl.when(e + 1 < n_experts)
    def _():
        nxt = jnp.minimum(e + 1, n_experts - 1)
        for cp in weight_copies(nxt, 1 - wslot):
            cp.start(priority=WEIGHT_DMA_PRIORITY)
        prefetch_first_chunk(nxt)

    for cp in weight_copies(e, wslot):
        cp.wait()

    def chunk_body(c, carry):
        slot = jnp.where(c == 0, wslot, extra_slot)

        @pl.when(c > 0)
        def _():
            x_copy(seg + c * tm, extra_slot).start()

        x_copy(0, slot).wait()
        x = jnp.concatenate([xbuf[slot, :, s, :] for s in range(ROW_TILES)], axis=1).astype(BF16)
        g = jnp.dot(x, wg_buf[wslot].astype(BF16), preferred_element_type=F32)
        u = jnp.dot(x, wu_buf[wslot].astype(BF16), preferred_element_type=F32)
        hmid = (_silu(g) * u).astype(BF16)
        yslot = pending_ref[1]
        ybuf[yslot] = jnp.dot(hmid, wd_buf[wslot].astype(BF16), preferred_element_type=F32)
        drain_output()
        y_copy(seg + c * tm, yslot).start()
        pending_ref[0] = 1
        pending_ref[1] = 1 - yslot
        return carry

    lax.fori_loop(0, n_chunks, chunk_body, 0)

    @pl.when(e == n_experts - 1)
    def _():
        drain_output()
        total = seg + _padded_count(count)
        ybuf[0] = jnp.zeros((tm, D_MODEL), F32)
        n_fill = (n_rows - total + tm - 1) // tm

        def fill_body(j, carry):
            fill = y_copy(jnp.minimum(total + j * tm, n_rows - tm), 0)
            fill.start()
            fill.wait()
            return carry

        lax.fori_loop(0, n_fill, fill_body, 0)


def _experts(seg, counts, xs, w_gate, w_up, w_down):
    n_rows = xs.shape[0]
    tm = EXPERT_TM
    hbm = pl.BlockSpec(memory_space=pl.ANY)
    grid_spec = pltpu.PrefetchScalarGridSpec(
        num_scalar_prefetch=2,
        grid=(N_EXPERTS,),
        in_specs=[hbm, hbm, hbm, hbm],
        out_specs=hbm,
        scratch_shapes=[pltpu.VMEM((3, tm, ROW_TILES, LANES), F32),
                        pltpu.VMEM((2, tm, D_MODEL), F32),
                        pltpu.VMEM((2, D_MODEL, EXPERT_FF), F32),
                        pltpu.VMEM((2, D_MODEL, EXPERT_FF), F32),
                        pltpu.VMEM((2, EXPERT_FF, D_MODEL), F32),
                        pltpu.SemaphoreType.DMA((3,)),
                        pltpu.SemaphoreType.DMA(()),
                        pltpu.SemaphoreType.DMA((3, 2)),
                        pltpu.SMEM((2,), jnp.int32)],
    )
    return pl.pallas_call(
        _experts_kernel,
        grid_spec=grid_spec,
        out_shape=jax.ShapeDtypeStruct((n_rows, D_MODEL), F32),
        compiler_params=_cparams(("arbitrary",)),
        name="experts",
    )(seg, counts, xs, w_gate, w_up, w_down)


def _y_copy(y_hbm, ybuf, sem, src_row, slot, kk, r):
    return pltpu.make_async_copy(y_hbm.at[pl.ds(src_row, 1), :],
                                 ybuf.at[slot, kk, pl.ds(r, 1), :], sem.at[slot])


def _combine_kernel(dest_ref, y_hbm, x_ref, gate_ref, wsg_ref, wsu_ref, wsd_ref,
                    gain_ref, bias_ref, o_ref, ybuf, sem):
    tm = COMBINE_TM
    i = pl.program_id(0)
    n_tiles = pl.num_programs(0)
    slot = i % 2

    group = SUBLANES
    n_tok = n_tiles * tm

    def issue(tile, dst_slot):
        def body(g, carry):
            for rr in range(group):
                r = g * group + rr
                tok = tile * tm + r
                for kk in range(TOP_K):
                    _y_copy(y_hbm, ybuf, sem, dest_ref[kk * n_tok + tok], dst_slot, kk, r).start(
                        priority=kk % N_DMA_QUEUES)
            return carry
        lax.fori_loop(0, tm // group, body, 0)

    def wait_all(dst_slot):
        def body(g, carry):
            for _ in range(group * TOP_K):
                _y_copy(y_hbm, ybuf, sem, 0, dst_slot, 0, 0).wait()
            return carry
        lax.fori_loop(0, tm // group, body, 0)

    @pl.when(i == 0)
    def _():
        issue(0, 0)

    @pl.when(i + 1 < n_tiles)
    def _():
        issue(i + 1, 1 - slot)

    x = x_ref[...]
    xb = x.astype(BF16)
    g = jnp.dot(xb, wsg_ref[...], preferred_element_type=F32)
    u = jnp.dot(xb, wsu_ref[...], preferred_element_type=F32)
    shared = jnp.dot((_silu(g) * u).astype(BF16), wsd_ref[...], preferred_element_type=F32)

    wait_all(slot)
    gate_t = gate_ref[...]
    diag = (lax.broadcasted_iota(jnp.int32, (tm, tm), 0)
            == lax.broadcasted_iota(jnp.int32, (tm, tm), 1))
    routed = jnp.zeros((tm, D_MODEL), F32)
    for kk in range(TOP_K):
        gate_col = jnp.sum(jnp.where(diag, gate_t[kk:kk + 1, :], 0.0), axis=1, keepdims=True)
        routed = routed + gate_col * ybuf[slot, kk]
    z = DEEPNORM_ALPHA * x + (routed + shared)
    o_ref[...] = _layer_norm(z, gain_ref[...], bias_ref[...])


def _combine(dest_flat, y_rows, x1, gate, wsg, wsu, wsd, gain, bias):
    n_tok = x1.shape[0]
    tm = COMBINE_TM
    row = lambda w: pl.BlockSpec((tm, w), lambda i, d: (i, 0))
    full = lambda r, w: pl.BlockSpec((r, w), lambda i, d: (0, 0))
    grid_spec = pltpu.PrefetchScalarGridSpec(
        num_scalar_prefetch=1,
        grid=(n_tok // tm,),
        in_specs=[pl.BlockSpec(memory_space=pl.ANY), row(D_MODEL),
                  pl.BlockSpec((TOP_K, tm), lambda i, d: (0, i)),
                  full(D_MODEL, SHARED_FF), full(D_MODEL, SHARED_FF), full(SHARED_FF, D_MODEL),
                  full(1, D_MODEL), full(1, D_MODEL)],
        out_specs=row(D_MODEL),
        scratch_shapes=[pltpu.VMEM((2, TOP_K, tm, D_MODEL), F32),
                        pltpu.SemaphoreType.DMA((2,))],
    )
    return pl.pallas_call(
        _combine_kernel,
        grid_spec=grid_spec,
        out_shape=jax.ShapeDtypeStruct((n_tok, D_MODEL), F32),
        compiler_params=_cparams(("arbitrary",)),
        name="combine",
    )(dest_flat, y_rows, x1, gate, wsg, wsu, wsd, gain, bias)


def _rope_tables(seq):
    half = HEAD_DIM // 2
    pos = jnp.arange(seq, dtype=F32)
    inv_freq = ROPE_BASE ** (-jnp.arange(half, dtype=F32) / half)
    ang = pos[:, None] * inv_freq[None, :]
    cos, sin = jnp.cos(ang), jnp.sin(ang)
    return jnp.concatenate([cos, cos], axis=-1), jnp.concatenate([-sin, sin], axis=-1)


def kernel(x, w_in, ret_gn_gain, hgrn_lb_logits, hgrn_norm_gain, w_out, ln1_gain, ln1_bias,
           w_router, router_bias, w_gate, w_up, w_down, ws_gate, ws_up, ws_down,
           ln2_gain, ln2_bias):
    batch, seq, d = x.shape
    n_tok = batch * seq
    cos_t, sin_t = _rope_tables(seq)
    lower_bounds = jnp.cumsum(jax.nn.softmax(hgrn_lb_logits.astype(F32), axis=0), axis=0)
    x2d = x.reshape(n_tok, d)
    for l in range(DEPTH):
        p, logf = _in_proj(x2d.astype(BF16), w_in[l].astype(BF16), cos_t, sin_t,
                           lower_bounds[l].reshape(1, GROUP_WIDTH), seq)
        ret_o = _retention(p, ret_gn_gain[l].reshape(1, GROUP_WIDTH), batch, seq)
        hg_o = _hgrn2(p, logf, hgrn_norm_gain[l].reshape(1, GROUP_WIDTH), batch, seq)
        x1 = _out_proj(ret_o, hg_o, w_out[l].astype(BF16), x2d,
                       ln1_gain[l].reshape(1, d), ln1_bias[l].reshape(1, d))
        gate, dest, counts, seg = _router(x1, w_router[l], router_bias[l].reshape(N_EXPERTS, 1))
        dest_flat = dest.reshape(TOP_K * n_tok)
        counts, seg = counts.reshape(N_EXPERTS), seg.reshape(N_EXPERTS)
        x_sorted = _dispatch(dest_flat, seg, counts, x1)
        y_rows = _experts(seg, counts, x_sorted, w_gate[l], w_up[l], w_down[l])
        x2d = _combine(dest_flat, y_rows, x1, gate, ws_gate[l].astype(BF16),
                       ws_up[l].astype(BF16), ws_down[l].astype(BF16),
                       ln2_gain[l].reshape(1, d), ln2_bias[l].reshape(1, d))
    return x2d.reshape(batch, seq, d)
```

```python
import functools

import numpy as np
import jax
import jax.numpy as jnp
from jax import lax
from jax.experimental import pallas as pl
from jax.experimental.pallas import tpu as pltpu

D_MODEL = 2048
N_HEADS = 8
HEAD_DIM = 128
GROUP_WIDTH = N_HEADS * HEAD_DIM
N_SECTIONS = 8
ROPE_BASE = 10000.0
N_EXPERTS = 256
TOP_K = 8
N_GROUPS = 8
GROUP_SIZE = N_EXPERTS // N_GROUPS
TOPK_GROUPS = 4
EXPERT_FF = 512
SHARED_FF = 512
ROUTED_SCALE = 2.5
NORM_EPS = 1e-5
DEPTH = 1
DEEPNORM_ALPHA = (2.0 * DEPTH) ** 0.25

V7X_VMEM_LIMIT_BYTES = 56 * 1024 * 1024
LANES = 128
SUBLANES = 8

INPROJ_TM = 1024
INPROJ_TN = 256
MIX_TS = 512
RET_CHUNK = 128
HG_CHUNK = 64
HG_DIRECT = 8
OUTPROJ_TM = 256
ROUTER_TM = 256
DISPATCH_TM = 256
EXPERT_TM = 288
WEIGHT_DMA_PRIORITY = 1
N_DMA_QUEUES = 2
ROW_PAD = SUBLANES
COMBINE_TM = 128

F32 = jnp.float32
BF16 = jnp.bfloat16


def _sigmoid(v):
    return 1.0 / (1.0 + jnp.exp(-v))


def _silu(v):
    return v * _sigmoid(v)


def _cparams(semantics):
    return pltpu.CompilerParams(dimension_semantics=semantics,
                                vmem_limit_bytes=V7X_VMEM_LIMIT_BYTES)


def _inproj_kernel(x_ref, w_ref, cos_ref, sin_ref, lb_ref, p_ref, logf_ref, xb_ref):
    j = pl.program_id(1)

    @pl.when(j == 0)
    def _():
        xb_ref[...] = x_ref[...].astype(BF16)

    x = xb_ref[...]

    def column_blocks():
        for c in range(GROUP_WIDTH // INPROJ_TN):
            cols = slice(c * INPROJ_TN, (c + 1) * INPROJ_TN)
            yield cols, jnp.dot(x, w_ref[:, cols], preferred_element_type=F32)

    @pl.when(j < 2)
    def _rotary():
        scale = jnp.where(j == 1, HEAD_DIM ** -0.5, 1.0).astype(F32)
        cos = cos_ref[...]
        sin = sin_ref[...]
        for cols, acc in column_blocks():
            for h in range(INPROJ_TN // HEAD_DIM):
                t = acc[:, h * HEAD_DIM:(h + 1) * HEAD_DIM]
                r = pltpu.roll(t, HEAD_DIM // 2, axis=1)
                lo = cols.start + h * HEAD_DIM
                p_ref[:, lo:lo + HEAD_DIM] = ((t * cos + r * sin) * scale).astype(BF16)

    @pl.when(j == 5)
    def _forget():
        for cols, acc in column_blocks():
            lb = lb_ref[:, cols]
            f = lb + (1.0 - lb) * _sigmoid(acc)
            logf_ref[:, cols] = jnp.log(f)
            p_ref[:, cols] = (1.0 - f).astype(BF16)

    @pl.when(jnp.logical_and(j >= 2, j != 5))
    def _plain():
        for cols, acc in column_blocks():
            p_ref[:, cols] = acc.astype(BF16)


def _in_proj(x2d, w_bf, cos_t, sin_t, lb, seq):
    n_tok = x2d.shape[0]
    tm = INPROJ_TM
    pos_blocks = seq // tm
    return pl.pallas_call(
        _inproj_kernel,
        grid=(n_tok // tm, N_SECTIONS),
        in_specs=[
            pl.BlockSpec((tm, D_MODEL), lambda i, j: (i, 0)),
            pl.BlockSpec((D_MODEL, GROUP_WIDTH), lambda i, j: (0, j)),
            pl.BlockSpec((tm, HEAD_DIM), lambda i, j: (i % pos_blocks, 0)),
            pl.BlockSpec((tm, HEAD_DIM), lambda i, j: (i % pos_blocks, 0)),
            pl.BlockSpec((1, GROUP_WIDTH), lambda i, j: (0, 0)),
        ],
        out_specs=[
            pl.BlockSpec((tm, GROUP_WIDTH), lambda i, j: (i, j)),
            pl.BlockSpec((tm, GROUP_WIDTH), lambda i, j: (i, 0)),
        ],
        out_shape=[
            jax.ShapeDtypeStruct((n_tok, N_SECTIONS * GROUP_WIDTH), BF16),
            jax.ShapeDtypeStruct((n_tok, GROUP_WIDTH), F32),
        ],
        scratch_shapes=[pltpu.VMEM((tm, D_MODEL), BF16)],
        compiler_params=_cparams(("arbitrary", "arbitrary")),
        name="in_proj",
    )(x2d, w_bf, cos_t, sin_t, lb)


def _ret_log_decay(h):
    return float(np.log(np.float32(1.0) - np.float32(2.0) ** np.float32(-5.0 - h)))


def _retention_kernel(q_ref, k_ref, v_ref, g_ref, gain_ref, o_ref,
                      decay_ref, qscale_ref, kscale_ref, *state_refs):
    c = RET_CHUNK

    @pl.when(pl.program_id(1) == 0)
    def _():
        for state_ref in state_refs:
            state_ref[...] = jnp.zeros_like(state_ref)

    row = lax.broadcasted_iota(jnp.int32, (c, c), 0)
    col = lax.broadcasted_iota(jnp.int32, (c, c), 1)
    causal = row >= col
    delta = jnp.where(causal, row - col, 0).astype(F32)
    rvec = lax.broadcasted_iota(jnp.int32, (c, HEAD_DIM), 0).astype(F32)
    for h in range(N_HEADS):
        ld = _ret_log_decay(h)
        decay_ref[h] = jnp.where(causal, jnp.exp(delta * ld), 0.0)
        qscale_ref[h] = jnp.exp((rvec + 1.0) * ld)
        kscale_ref[h] = jnp.exp((c - 1.0 - rvec) * ld)

    def chunk_body(ci, carry):
        rows = pl.ds(pl.multiple_of(ci * c, c), c)
        for h in range(N_HEADS):
            hs = slice(h * HEAD_DIM, (h + 1) * HEAD_DIM)
            chunk_decay = float(np.exp(np.float32(_ret_log_decay(h)) * np.float32(c)))
            q = q_ref[rows, hs]
            k = k_ref[rows, hs]
            v = v_ref[rows, hs]
            state = state_refs[h][...]
            scores = lax.dot_general(q, k, (((1,), (1,)), ((), ())),
                                     preferred_element_type=F32) * decay_ref[h]
            o = jnp.dot(scores.astype(BF16), v, preferred_element_type=F32)
            o = o + qscale_ref[h] * jnp.dot(q, state.astype(BF16), preferred_element_type=F32)
            k_dec = (k.astype(F32) * kscale_ref[h]).astype(BF16)
            kv = lax.dot_general(k_dec, v, (((0,), (0,)), ((), ())),
                                 preferred_element_type=F32)
            state_refs[h][...] = chunk_decay * state + kv
            o = o - jnp.mean(o, axis=-1, keepdims=True)
            o = o * lax.rsqrt(jnp.mean(o * o, axis=-1, keepdims=True) + NORM_EPS)
            gate = g_ref[rows, hs].astype(F32)
            o_ref[rows, hs] = (o * gain_ref[:, hs] * _silu(gate)).astype(BF16)
        return carry

    lax.fori_loop(0, MIX_TS // c, chunk_body, 0)


def _retention(p, gain, batch, seq):
    n_tok = p.shape[0]
    ts = MIX_TS
    tpb = seq // ts
    c = RET_CHUNK
    sec = lambda s: pl.BlockSpec((ts, GROUP_WIDTH), lambda b, t, s=s: (b * tpb + t, s))
    return pl.pallas_call(
        _retention_kernel,
        grid=(batch, tpb),
        in_specs=[sec(0), sec(1), sec(2), sec(3),
                  pl.BlockSpec((1, GROUP_WIDTH), lambda b, t: (0, 0))],
        out_specs=pl.BlockSpec((ts, GROUP_WIDTH), lambda b, t: (b * tpb + t, 0)),
        out_shape=jax.ShapeDtypeStruct((n_tok, GROUP_WIDTH), BF16),
        scratch_shapes=[pltpu.VMEM((N_HEADS, c, c), F32),
                        pltpu.VMEM((N_HEADS, c, HEAD_DIM), F32),
                        pltpu.VMEM((N_HEADS, c, HEAD_DIM), F32)]
                       + [pltpu.VMEM((HEAD_DIM, HEAD_DIM), F32) for _ in range(N_HEADS)],
        compiler_params=_cparams(("arbitrary", "arbitrary")),
        name="retention",
    )(p, p, p, p, gain)


def _hgrn2_kernel(q_ref, k_ref, v_ref, g_ref, logf_ref, gain_ref, o_ref, state_ref, cum_ref):
    c = HG_CHUNK
    ts = MIX_TS

    @pl.when(pl.program_id(1) == 0)
    def _():
        state_ref[...] = jnp.zeros_like(state_ref)

    trow = lax.broadcasted_iota(jnp.int32, (ts, 1), 0) % c
    cum = logf_ref[...]
    sh = 1
    while sh < c:
        cum = cum + jnp.where(trow >= sh, pltpu.roll(cum, sh, axis=0), 0.0)
        sh *= 2
    cum_ref[...] = cum

    crow = lax.broadcasted_iota(jnp.int32, (c, 1), 0)
    srow = lax.broadcasted_iota(jnp.int32, (c, c), 0)
    scol = lax.broadcasted_iota(jnp.int32, (c, c), 1)
    drow = lax.broadcasted_iota(jnp.int32, (HG_DIRECT, 1), 0)
    ones_bf = jnp.ones((HEAD_DIM, HEAD_DIM), BF16)
    n_direct = c // HG_DIRECT

    def chunk_body(ci, carry):
        rows = pl.ds(pl.multiple_of(ci * c, c), c)
        for h in range(N_HEADS):
            hs = slice(h * HEAD_DIM, (h + 1) * HEAD_DIM)
            q = q_ref[rows, hs].astype(F32)
            k = k_ref[rows, hs].astype(F32)
            v_bf = v_ref[rows, hs]
            v = v_bf.astype(F32)
            b = cum_ref[rows, hs]
            b_last = b[c - 1:c, :]
            state_t = state_ref[h]

            q_in = (q * jnp.exp(b)).astype(BF16)
            o = lax.dot_general(q_in, state_t.astype(BF16), (((1,), (1,)), ((), ())),
                                preferred_element_type=F32)

            scores = jnp.zeros((c, c), F32)
            level = 2 * HG_DIRECT
            while level <= c:
                half = level // 2
                anchors = []
                for blk in range(c // level):
                    a = blk * level + half - 1
                    anchors.append(jnp.broadcast_to(b[a:a + 1, :], (level, HEAD_DIM)))
                anchor = anchors[0] if len(anchors) == 1 else jnp.concatenate(anchors, axis=0)
                upper = (crow % level) >= half
                q_l = jnp.where(upper, q * jnp.exp(jnp.minimum(b - anchor, 0.0)), 0.0)
                k_l = jnp.where(upper, 0.0, k * jnp.exp(jnp.minimum(anchor - b, 0.0)))
                s_l = lax.dot_general(q_l.astype(BF16), k_l.astype(BF16),
                                      (((1,), (1,)), ((), ())), preferred_element_type=F32)
                same_block = (srow // level) == (scol // level)
                scores = scores + jnp.where(same_block, s_l, 0.0)
                level *= 2
            o = o + jnp.dot(scores.astype(BF16), v_bf, preferred_element_type=F32)

            prods = []
            for blk in range(n_direct):
                r0 = blk * HG_DIRECT
                qb = q[r0:r0 + HG_DIRECT, :]
                bb = b[r0:r0 + HG_DIRECT, :]
                for s in range(HG_DIRECT):
                    dec = jnp.exp(jnp.where(drow >= s, bb - b[r0 + s:r0 + s + 1, :], -jnp.inf))
                    prods.append(qb * dec * k[r0 + s:r0 + s + 1, :])
            prod = jnp.concatenate(prods, axis=0).astype(BF16)
            rowsum = jnp.dot(prod, ones_bf, preferred_element_type=F32)
            direct = []
            for blk in range(n_direct):
                r0 = blk * HG_DIRECT
                acc = jnp.zeros((HG_DIRECT, HEAD_DIM), F32)
                for s in range(HG_DIRECT):
                    p0 = (blk * HG_DIRECT + s) * HG_DIRECT
                    acc = acc + rowsum[p0:p0 + HG_DIRECT, :] * v[r0 + s:r0 + s + 1, :]
                direct.append(acc)
            o = o + jnp.concatenate(direct, axis=0)

            k_dec = (k * jnp.exp(b_last - b)).astype(BF16)
            vk = lax.dot_general(v_bf, k_dec, (((0,), (0,)), ((), ())),
                                 preferred_element_type=F32)
            state_ref[h] = state_t * jnp.exp(b_last) + vk

            o = o * lax.rsqrt(jnp.mean(o * o, axis=-1, keepdims=True) + NORM_EPS)
            gate = g_ref[rows, hs].astype(F32)
            o_ref[rows, hs] = (o * gain_ref[:, hs] * _silu(gate)).astype(BF16)
        return carry

    lax.fori_loop(0, ts // c, chunk_body, 0)


def _hgrn2(p, logf, gain, batch, seq):
    n_tok = p.shape[0]
    ts = MIX_TS
    tpb = seq // ts
    sec = lambda s: pl.BlockSpec((ts, GROUP_WIDTH), lambda b, t, s=s: (b * tpb + t, s))
    return pl.pallas_call(
        _hgrn2_kernel,
        grid=(batch, tpb),
        in_specs=[sec(4), sec(5), sec(6), sec(7),
                  pl.BlockSpec((ts, GROUP_WIDTH), lambda b, t: (b * tpb + t, 0)),
                  pl.BlockSpec((1, GROUP_WIDTH), lambda b, t: (0, 0))],
        out_specs=pl.BlockSpec((ts, GROUP_WIDTH), lambda b, t: (b * tpb + t, 0)),
        out_shape=jax.ShapeDtypeStruct((n_tok, GROUP_WIDTH), BF16),
        scratch_shapes=[pltpu.VMEM((N_HEADS, HEAD_DIM, HEAD_DIM), F32),
                        pltpu.VMEM((ts, GROUP_WIDTH), F32)],
        compiler_params=_cparams(("arbitrary", "arbitrary")),
        name="hgrn2",
    )(p, p, p, p, logf, gain)


def _layer_norm(z, gain, bias):
    mu = jnp.mean(z, axis=-1, keepdims=True)
    zc = z - mu
    var = jnp.mean(zc * zc, axis=-1, keepdims=True)
    return zc * lax.rsqrt(var + NORM_EPS) * gain + bias


def _outproj_kernel(ret_ref, hg_ref, w_ref, x_ref, gain_ref, bias_ref, o_ref):
    mix = jnp.dot(ret_ref[...], w_ref[0:GROUP_WIDTH, :], preferred_element_type=F32)
    mix = mix + jnp.dot(hg_ref[...], w_ref[GROUP_WIDTH:2 * GROUP_WIDTH, :],
                        preferred_element_type=F32)
    z = DEEPNORM_ALPHA * x_ref[...] + mix
    o_ref[...] = _layer_norm(z, gain_ref[...], bias_ref[...])


def _out_proj(ret_o, hg_o, w_bf, x2d, gain, bias):
    n_tok = x2d.shape[0]
    tm = OUTPROJ_TM
    row = lambda w: pl.BlockSpec((tm, w), lambda i: (i, 0))
    full = lambda r, w: pl.BlockSpec((r, w), lambda i: (0, 0))
    return pl.pallas_call(
        _outproj_kernel,
        grid=(n_tok // tm,),
        in_specs=[row(GROUP_WIDTH), row(GROUP_WIDTH), full(2 * GROUP_WIDTH, D_MODEL),
                  row(D_MODEL), full(1, D_MODEL), full(1, D_MODEL)],
        out_specs=row(D_MODEL),
        out_shape=jax.ShapeDtypeStruct((n_tok, D_MODEL), F32),
        compiler_params=_cparams(("arbitrary",)),
        name="out_proj",
    )(ret_o, hg_o, w_bf, x2d, gain, bias)


def _router_kernel(x_ref, w_ref, bias_ref, gate_ref, dest_ref, count_ref, seg_ref,
                   base_ref, eidx_scr, rank_scr):
    tm = ROUTER_TM
    step = pl.program_id(0)
    tok_cols = pl.ds(pl.multiple_of(step * tm, tm), tm)

    @pl.when(step == 0)
    def _():
        base_ref[...] = jnp.zeros_like(base_ref)

    logits = jnp.dot(x_ref[...], w_ref[...], preferred_element_type=F32,
                     precision=lax.Precision.HIGHEST)
    scores = _sigmoid(logits.T)
    sel = scores + bias_ref[...]
    neg = -jnp.inf
    erow = lax.broadcasted_iota(jnp.int32, (N_EXPERTS, tm), 0).astype(F32)
    grow = lax.broadcasted_iota(jnp.int32, (GROUP_SIZE, tm), 0).astype(F32)

    def first_argmax(vals, rows, sentinel):
        m = jnp.max(vals, axis=0, keepdims=True)
        idx = jnp.min(jnp.where(vals == m, rows, sentinel), axis=0, keepdims=True)
        return m, idx

    group_vals, group_score = [], []
    for g in range(N_GROUPS):
        vals = sel[g * GROUP_SIZE:(g + 1) * GROUP_SIZE, :]
        m1, i1 = first_argmax(vals, grow, float(GROUP_SIZE))
        m2 = jnp.max(jnp.where(grow == i1, neg, vals), axis=0, keepdims=True)
        group_vals.append(vals)
        group_score.append(m1 + m2)

    kept = []
    for g in range(N_GROUPS):
        beaten_by = jnp.zeros((1, tm), F32)
        for o in range(N_GROUPS):
            if o == g:
                continue
            wins = (group_score[o] >= group_score[g]) if o < g else (group_score[o] > group_score[g])
            beaten_by = beaten_by + jnp.where(wins, 1.0, 0.0)
        kept.append(jnp.where(beaten_by < TOPK_GROUPS, group_vals[g], neg))
    cur = jnp.concatenate(kept, axis=0)

    idx_rows, gate_rows, onehots = [], [], []
    for kk in range(TOP_K):
        _, idx = first_argmax(cur, erow, float(N_EXPERTS))
        onehot = erow == idx
        gate_rows.append(jnp.sum(jnp.where(onehot, scores, 0.0), axis=0, keepdims=True))
        cur = jnp.where(onehot, neg, cur)
        onehots.append(onehot)
        idx_rows.append(idx)
    gate_t = jnp.concatenate(gate_rows, axis=0)
    gate_ref[...] = gate_t / jnp.sum(gate_t, axis=0, keepdims=True) * ROUTED_SCALE
    eidx_scr[:, tok_cols] = jnp.concatenate(idx_rows, axis=0)

    multihot = jnp.zeros((N_EXPERTS, tm), F32)
    for onehot in onehots:
        multihot = multihot + jnp.where(onehot, 1.0, 0.0)
    trow = lax.broadcasted_iota(jnp.int32, (tm, tm), 0)
    tcol = lax.broadcasted_iota(jnp.int32, (tm, tm), 1)
    earlier = jnp.where(trow < tcol, 1.0, 0.0).astype(BF16)
    before = jnp.dot(multihot.astype(BF16), earlier, preferred_element_type=F32)
    pos = before + base_ref[...]
    rank_rows = [jnp.sum(jnp.where(onehot, pos, 0.0), axis=0, keepdims=True) for onehot in onehots]
    rank_scr[:, tok_cols] = jnp.concatenate(rank_rows, axis=0)
    new_base = base_ref[...] + jnp.sum(multihot, axis=1, keepdims=True)
    base_ref[...] = new_base

    @pl.when(step == pl.num_programs(0) - 1)
    def _():
        counts = new_base
        padded = jnp.floor((counts + (ROW_PAD - 1)) / ROW_PAD) * ROW_PAD
        hi = jnp.floor(padded / 256.0)
        lo = padded - 256.0 * hi
        prow = lax.broadcasted_iota(jnp.int32, (N_EXPERTS, N_EXPERTS), 0)
        pcol = lax.broadcasted_iota(jnp.int32, (N_EXPERTS, N_EXPERTS), 1)
        before_e = jnp.where(pcol < prow, 1.0, 0.0).astype(BF16)
        bcast = lambda v: jnp.broadcast_to(v, (N_EXPERTS, LANES)).astype(BF16)
        seg = (256.0 * jnp.dot(before_e, bcast(hi), preferred_element_type=F32)
               + jnp.dot(before_e, bcast(lo), preferred_element_type=F32))[:, 0:1]
        count_ref[...] = counts.astype(jnp.int32)
        seg_ref[...] = seg.astype(jnp.int32)

        def tile_body(t, carry):
            cols = pl.ds(pl.multiple_of(t * tm, tm), tm)
            eidx = eidx_scr[:, cols]
            starts = [jnp.sum(jnp.where(erow == eidx[kk:kk + 1, :], seg, 0.0), axis=0, keepdims=True)
                      for kk in range(TOP_K)]
            dest = jnp.concatenate(starts, axis=0) + rank_scr[:, cols]
            dest_ref[:, cols] = dest.astype(jnp.int32)
            return carry

        lax.fori_loop(0, pl.num_programs(0), tile_body, 0)


def _router(x1, w_router, bias):
    n_tok = x1.shape[0]
    tm = ROUTER_TM
    row = lambda w: pl.BlockSpec((tm, w), lambda i: (i, 0))
    full = lambda r, w: pl.BlockSpec((r, w), lambda i: (0, 0))
    return pl.pallas_call(
        _router_kernel,
        grid=(n_tok // tm,),
        in_specs=[row(D_MODEL), full(D_MODEL, N_EXPERTS), full(N_EXPERTS, 1)],
        out_specs=[pl.BlockSpec((TOP_K, tm), lambda i: (0, i)), full(TOP_K, n_tok),
                   full(N_EXPERTS, 1), full(N_EXPERTS, 1)],
        out_shape=[jax.ShapeDtypeStruct((TOP_K, n_tok), F32),
                   jax.ShapeDtypeStruct((TOP_K, n_tok), jnp.int32),
                   jax.ShapeDtypeStruct((N_EXPERTS, 1), jnp.int32),
                   jax.ShapeDtypeStruct((N_EXPERTS, 1), jnp.int32)],
        scratch_shapes=[pltpu.VMEM((N_EXPERTS, 1), F32),
                        pltpu.VMEM((TOP_K, n_tok), F32),
                        pltpu.VMEM((TOP_K, n_tok), F32)],
        compiler_params=_cparams(("arbitrary",)),
        name="router",
    )(x1, w_router, bias)


def _sorted_rows(n_tok):
    return n_tok * TOP_K + N_EXPERTS * ROW_PAD + EXPERT_TM


def _padded_count(count):
    return (count + (ROW_PAD - 1)) // ROW_PAD * ROW_PAD


def _dispatch_row_copy(x_ref, xs_hbm, sem, row, dst_row):
    return pltpu.make_async_copy(x_ref.at[pl.ds(row, 1), :], xs_hbm.at[pl.ds(dst_row, 1), :], sem)


def _dispatch_kernel(dest_ref, seg_ref, count_ref, x_ref, wsg_ref, wsu_ref, wsd_ref,
                     xs_hbm, shared_ref, zero_buf, sem, zsem):
    tm = DISPATCH_TM
    i = pl.program_id(0)
    group = SUBLANES

    @pl.when(i == 0)
    def _():
        zero_buf[...] = jnp.zeros_like(zero_buf)

        def pad_copy(e, j):
            row = seg_ref[e] + count_ref[e] + j
            return pltpu.make_async_copy(zero_buf.at[pl.ds(0, 1), :], xs_hbm.at[pl.ds(row, 1), :], zsem)

        def pad_body(e, carry):
            n_pad = _padded_count(count_ref[e]) - count_ref[e]
            for j in range(ROW_PAD - 1):
                @pl.when(j < n_pad)
                def _(j=j):
                    pad_copy(e, j).start()
            for j in range(ROW_PAD - 1):
                @pl.when(j < n_pad)
                def _(j=j):
                    pad_copy(e, j).wait()
            return carry

        lax.fori_loop(0, N_EXPERTS, pad_body, 0)
        last = N_EXPERTS - 1
        n_rows = xs_hbm.shape[0]
        total = seg_ref[last] + _padded_count(count_ref[last])

        def tail_body(j, carry):
            row0 = pl.multiple_of(jnp.minimum(total + j * EXPERT_TM, n_rows - EXPERT_TM), ROW_PAD)
            tail = pltpu.make_async_copy(zero_buf, xs_hbm.at[pl.ds(row0, EXPERT_TM), :], zsem)
            tail.start()
            tail.wait()
            return carry

        lax.fori_loop(0, (n_rows - total + EXPERT_TM - 1) // EXPERT_TM, tail_body, 0)

    n_tok = pl.num_programs(0) * tm

    def issue_body(g, carry):
        for rr in range(group):
            row = g * group + rr
            tok = i * tm + row
            for kk in range(TOP_K):
                _dispatch_row_copy(x_ref, xs_hbm, sem, row, dest_ref[kk * n_tok + tok]).start(
                    priority=kk % N_DMA_QUEUES)
        return carry

    lax.fori_loop(0, tm // group, issue_body, 0)

    xb = x_ref[...].astype(BF16)
    g = jnp.dot(xb, wsg_ref[...], preferred_element_type=F32)
    u = jnp.dot(xb, wsu_ref[...], preferred_element_type=F32)
    shared_ref[...] = jnp.dot((_silu(g) * u).astype(BF16), wsd_ref[...],
                              preferred_element_type=F32)

    def wait_body(g, carry):
        for _ in range(group * TOP_K):
            _dispatch_row_copy(x_ref, xs_hbm, sem, 0, 0).wait()
        return carry

    lax.fori_loop(0, tm // group, wait_body, 0)


def _dispatch(dest_flat, seg, counts, x1, wsg, wsu, wsd):
    n_tok = x1.shape[0]
    tm = DISPATCH_TM
    row = pl.BlockSpec((tm, D_MODEL), lambda i, d, s, c: (i, 0))
    full = lambda r, w: pl.BlockSpec((r, w), lambda i, d, s, c: (0, 0))
    grid_spec = pltpu.PrefetchScalarGridSpec(
        num_scalar_prefetch=3,
        grid=(n_tok // tm,),
        in_specs=[row, full(D_MODEL, SHARED_FF), full(D_MODEL, SHARED_FF), full(SHARED_FF, D_MODEL)],
        out_specs=[pl.BlockSpec(memory_space=pl.ANY), row],
        scratch_shapes=[pltpu.VMEM((EXPERT_TM, D_MODEL), F32),
                        pltpu.SemaphoreType.DMA(()),
                        pltpu.SemaphoreType.DMA(())],
    )
    return pl.pallas_call(
        _dispatch_kernel,
        grid_spec=grid_spec,
        out_shape=[jax.ShapeDtypeStruct((_sorted_rows(n_tok), D_MODEL), F32),
                   jax.ShapeDtypeStruct((n_tok, D_MODEL), F32)],
        compiler_params=_cparams(("arbitrary",)),
        name="dispatch",
    )(dest_flat, seg, counts, x1, wsg, wsu, wsd)


def _experts_kernel(seg_ref, count_ref, xs_hbm, wg_hbm, wu_hbm, wd_hbm, y_hbm,
                    xbuf, ybuf, wg_buf, wu_buf, wd_buf, in_sem, out_sem, w_sem, pending_ref):
    tm = EXPERT_TM
    e = pl.program_id(0)
    n_experts = pl.num_programs(0)
    n_rows = y_hbm.shape[0]
    count = count_ref[e]
    seg = seg_ref[e]
    n_chunks = (count + tm - 1) // tm
    wslot = e % 2
    extra_slot = 2

    def x_copy(row0, slot):
        return pltpu.make_async_copy(xs_hbm.at[pl.ds(pl.multiple_of(row0, ROW_PAD), tm), :],
                                     xbuf.at[slot], in_sem.at[slot])

    def y_copy(row0, slot):
        return pltpu.make_async_copy(ybuf.at[slot],
                                     y_hbm.at[pl.ds(pl.multiple_of(row0, ROW_PAD), tm), :], out_sem)

    def weight_copies(expert, slot):
        return [pltpu.make_async_copy(src.at[expert], dst.at[slot], w_sem.at[m, slot])
                for m, (src, dst) in enumerate(((wg_hbm, wg_buf), (wu_hbm, wu_buf), (wd_hbm, wd_buf)))]

    def drain_output():
        @pl.when(pending_ref[0] == 1)
        def _():
            y_copy(0, 0).wait()
            pending_ref[0] = 0

    def prefetch_first_chunk(expert):
        @pl.when(count_ref[expert] > 0)
        def _():
            x_copy(seg_ref[expert], expert % 2).start()

    @pl.when(e == 0)
    def _():
        pending_ref[0] = 0
        pending_ref[1] = 0
        for cp in weight_copies(0, 0):
            cp.start(priority=WEIGHT_DMA_PRIORITY)
        prefetch_first_chunk(0)

    @pl.when(e + 1 < n_experts)
    def _():
        nxt = jnp.minimum(e + 1, n_experts - 1)
        for cp in weight_copies(nxt, 1 - wslot):
            cp.start(priority=WEIGHT_DMA_PRIORITY)
        prefetch_first_chunk(nxt)

    for cp in weight_copies(e, wslot):
        cp.wait()

    def chunk_body(c, carry):
        slot = jnp.where(c == 0, wslot, extra_slot)

        @pl.when(c > 0)
        def _():
            x_copy(seg + c * tm, extra_slot).start()

        x_copy(0, slot).wait()
        x = xbuf[slot].astype(BF16)
        g = jnp.dot(x, wg_buf[wslot].astype(BF16), preferred_element_type=F32)
        u = jnp.dot(x, wu_buf[wslot].astype(BF16), preferred_element_type=F32)
        hmid = (_silu(g) * u).astype(BF16)
        yslot = pending_ref[1]
        ybuf[yslot] = jnp.dot(hmid, wd_buf[wslot].astype(BF16), preferred_element_type=F32)
        drain_output()
        y_copy(seg + c * tm, yslot).start()
        pending_ref[0] = 1
        pending_ref[1] = 1 - yslot
        return carry

    lax.fori_loop(0, n_chunks, chunk_body, 0)

    @pl.when(e == n_experts - 1)
    def _():
        drain_output()
        total = seg + _padded_count(count)
        ybuf[0] = jnp.zeros((tm, D_MODEL), F32)
        n_fill = (n_rows - total + tm - 1) // tm

        def fill_body(j, carry):
            fill = y_copy(jnp.minimum(total + j * tm, n_rows - tm), 0)
            fill.start()
            fill.wait()
            return carry

        lax.fori_loop(0, n_fill, fill_body, 0)


def _experts(seg, counts, xs, w_gate, w_up, w_down):
    n_rows = xs.shape[0]
    tm = EXPERT_TM
    hbm = pl.BlockSpec(memory_space=pl.ANY)
    grid_spec = pltpu.PrefetchScalarGridSpec(
        num_scalar_prefetch=2,
        grid=(N_EXPERTS,),
        in_specs=[hbm, hbm, hbm, hbm],
        out_specs=hbm,
        scratch_shapes=[pltpu.VMEM((3, tm, D_MODEL), F32),
                        pltpu.VMEM((2, tm, D_MODEL), F32),
                        pltpu.VMEM((2, D_MODEL, EXPERT_FF), F32),
                        pltpu.VMEM((2, D_MODEL, EXPERT_FF), F32),
                        pltpu.VMEM((2, EXPERT_FF, D_MODEL), F32),
                        pltpu.SemaphoreType.DMA((3,)),
                        pltpu.SemaphoreType.DMA(()),
                        pltpu.SemaphoreType.DMA((3, 2)),
                        pltpu.SMEM((2,), jnp.int32)],
    )
    return pl.pallas_call(
        _experts_kernel,
        grid_spec=grid_spec,
        out_shape=jax.ShapeDtypeStruct((n_rows, D_MODEL), F32),
        compiler_params=_cparams(("arbitrary",)),
        name="experts",
    )(seg, counts, xs, w_gate, w_up, w_down)


def _y_copy(y_hbm, ybuf, sem, src_row, slot, kk, r):
    return pltpu.make_async_copy(y_hbm.at[pl.ds(src_row, 1), :],
                                 ybuf.at[slot, kk, pl.ds(r, 1), :], sem.at[slot])


def _combine_kernel(dest_ref, y_hbm, x_ref, shared_ref, gate_ref, gain_ref, bias_ref, o_ref,
                    ybuf, sem):
    tm = COMBINE_TM
    i = pl.program_id(0)
    n_tiles = pl.num_programs(0)
    slot = i % 2

    group = SUBLANES
    n_tok = n_tiles * tm

    def issue(tile, dst_slot):
        def body(g, carry):
            for rr in range(group):
                r = g * group + rr
                tok = tile * tm + r
                for kk in range(TOP_K):
                    _y_copy(y_hbm, ybuf, sem, dest_ref[kk * n_tok + tok], dst_slot, kk, r).start(
                        priority=kk % N_DMA_QUEUES)
            return carry
        lax.fori_loop(0, tm // group, body, 0)

    def wait_all(dst_slot):
        def body(g, carry):
            for _ in range(group * TOP_K):
                _y_copy(y_hbm, ybuf, sem, 0, dst_slot, 0, 0).wait()
            return carry
        lax.fori_loop(0, tm // group, body, 0)

    @pl.when(i == 0)
    def _():
        issue(0, 0)

    for next_slot in range(2):
        @pl.when(jnp.logical_and(i + 1 < n_tiles, slot == 1 - next_slot))
        def _(next_slot=next_slot):
            issue(i + 1, next_slot)

    wait_all(slot)
    gate_t = gate_ref[...]
    diag = (lax.broadcasted_iota(jnp.int32, (tm, tm), 0)
            == lax.broadcasted_iota(jnp.int32, (tm, tm), 1))
    routed = jnp.zeros((tm, D_MODEL), F32)
    for kk in range(TOP_K):
        gate_col = jnp.sum(jnp.where(diag, gate_t[kk:kk + 1, :], 0.0), axis=1, keepdims=True)
        routed = routed + gate_col * ybuf[slot, kk]
    z = DEEPNORM_ALPHA * x_ref[...] + (routed + shared_ref[...])
    o_ref[...] = _layer_norm(z, gain_ref[...], bias_ref[...])


def _combine(dest_flat, y_rows, x1, shared, gate, gain, bias):
    n_tok = x1.shape[0]
    tm = COMBINE_TM
    row = lambda w: pl.BlockSpec((tm, w), lambda i, d: (i, 0))
    full = lambda r, w: pl.BlockSpec((r, w), lambda i, d: (0, 0))
    grid_spec = pltpu.PrefetchScalarGridSpec(
        num_scalar_prefetch=1,
        grid=(n_tok // tm,),
        in_specs=[pl.BlockSpec(memory_space=pl.ANY), row(D_MODEL), row(D_MODEL),
                  pl.BlockSpec((TOP_K, tm), lambda i, d: (0, i)),
                  full(1, D_MODEL), full(1, D_MODEL)],
        out_specs=row(D_MODEL),
        scratch_shapes=[pltpu.VMEM((2, TOP_K, tm, D_MODEL), F32),
                        pltpu.SemaphoreType.DMA((2,))],
    )
    return pl.pallas_call(
        _combine_kernel,
        grid_spec=grid_spec,
        out_shape=jax.ShapeDtypeStruct((n_tok, D_MODEL), F32),
        compiler_params=_cparams(("arbitrary",)),
        name="combine",
    )(dest_flat, y_rows, x1, shared, gate, gain, bias)


def _rope_tables(seq):
    half = HEAD_DIM // 2
    pos = jnp.arange(seq, dtype=F32)
    inv_freq = ROPE_BASE ** (-jnp.arange(half, dtype=F32) / half)
    ang = pos[:, None] * inv_freq[None, :]
    cos, sin = jnp.cos(ang), jnp.sin(ang)
    return jnp.concatenate([cos, cos], axis=-1), jnp.concatenate([-sin, sin], axis=-1)


def kernel(x, w_in, ret_gn_gain, hgrn_lb_logits, hgrn_norm_gain, w_out, ln1_gain, ln1_bias,
           w_router, router_bias, w_gate, w_up, w_down, ws_gate, ws_up, ws_down,
           ln2_gain, ln2_bias):
    batch, seq, d = x.shape
    n_tok = batch * seq
    cos_t, sin_t = _rope_tables(seq)
    lower_bounds = jnp.cumsum(jax.nn.softmax(hgrn_lb_logits.astype(F32), axis=0), axis=0)
    x2d = x.reshape(n_tok, d)
    for l in range(DEPTH):
        p, logf = _in_proj(x2d, w_in[l].astype(BF16), cos_t, sin_t,
                           lower_bounds[l].reshape(1, GROUP_WIDTH), seq)
        ret_o = _retention(p, ret_gn_gain[l].reshape(1, GROUP_WIDTH), batch, seq)
        hg_o = _hgrn2(p, logf, hgrn_norm_gain[l].reshape(1, GROUP_WIDTH), batch, seq)
        x1 = _out_proj(ret_o, hg_o, w_out[l].astype(BF16), x2d,
                       ln1_gain[l].reshape(1, d), ln1_bias[l].reshape(1, d))
        gate, dest, counts, seg = _router(x1, w_router[l], router_bias[l].reshape(N_EXPERTS, 1))
        dest_flat = dest.reshape(TOP_K * n_tok)
        counts, seg = counts.reshape(N_EXPERTS), seg.reshape(N_EXPERTS)
        x_sorted, shared = _dispatch(dest_flat, seg, counts, x1, ws_gate[l].astype(BF16),
                                     ws_up[l].astype(BF16), ws_down[l].astype(BF16))
        y_rows = _experts(seg, counts, x_sorted, w_gate[l], w_up[l], w_down[l])
        x2d = _combine(dest_flat, y_rows, x1, shared, gate,
                       ln2_gain[l].reshape(1, d), ln2_bias[l].reshape(1, d))
    return x2d.reshape(batch, seq, d)
```
